```python
import math
import jax, jax.numpy as jnp
from jax import lax
import numpy as np

D_MODEL = 1024
BATCH = 8
SEQ = 2048
DEPTH = 1
DEC_BATCH = 16
DEC_SEQ = 2048
PAST_LEN = 128

MIX_WIDTH = D_MODEL
RET_WIDTH = MIX_WIDTH // 2
SSM_WIDTH = MIX_WIDTH - RET_WIDTH
RET_HEADS = 4
RET_HEAD_DIM = RET_WIDTH // RET_HEADS
RET_CHUNK = 128
ROPE_BASE = 10000.0
SSM_GROUP = 16
SSM_GROUPS = SSM_WIDTH // SSM_GROUP
SSM_STATE = 64
DT_MIN = 1e-3
DT_MAX = 1e-1
D_FF = 2816
CONV_WIDTH = 3
EPS = 1e-6
IN_WIDTH = 4 * RET_WIDTH + SSM_WIDTH

kernel_name = "hybrid_retention_s5_encoder"


def _rms_norm(x, g):
    xf = x.astype(jnp.float32)
    y = xf * lax.rsqrt(jnp.mean(xf * xf, axis=-1, keepdims=True) + EPS)
    return (y * g.astype(jnp.float32)).astype(x.dtype)


def _rotary(x, pos):
    half = x.shape[-1] // 2
    inv_freq = ROPE_BASE ** (-jnp.arange(half, dtype=jnp.float32) / half)
    ang = pos[:, None] * inv_freq[None, :]
    cos = jnp.cos(ang)[None, :, None, :]
    sin = jnp.sin(ang)[None, :, None, :]
    x1, x2 = x[..., :half], x[..., half:]
    return jnp.concatenate([x1 * cos - x2 * sin, x1 * sin + x2 * cos], axis=-1)


def _retention_one_direction(q, k, v, log_gamma, strict):
    b, h, l, dh = q.shape
    n = l // RET_CHUNK
    c = RET_CHUNK
    qc = q.reshape(b, h, n, c, dh)
    kc = k.reshape(b, h, n, c, dh)
    vc = v.reshape(b, h, n, c, dh)
    idx = jnp.arange(c, dtype=jnp.float32)
    diff = idx[:, None] - idx[None, :]
    mask = (diff > 0) if strict else (diff >= 0)
    lg = log_gamma[:, None, None]
    inner_decay = jnp.where(mask[None], jnp.exp(lg * jnp.maximum(diff, 0.0)[None]), 0.0)
    scores = jnp.einsum('bhncd,bhnmd->bhncm', qc, kc) * inner_decay[None, :, None]
    inner = jnp.einsum('bhncm,bhnme->bhnce', scores, vc)
    zeta = jnp.exp(log_gamma[:, None] * (c - 1.0 - idx)[None, :])
    kv = jnp.einsum('bhncd,bhnce->nbhde', kc * zeta[None, :, None, :, None], vc)
    chunk_decay = jnp.exp(log_gamma * c)[None, :, None, None]

    def step(r, kv_n):
        return chunk_decay * r + kv_n, r

    _, r_prev = lax.scan(step, jnp.zeros((b, h, dh, dh), jnp.float32), kv)
    xi = jnp.exp(log_gamma[:, None] * (idx + 1.0)[None, :])
    cross = jnp.einsum('bhncd,nbhde->bhnce', qc, r_prev) * xi[None, :, None, :, None]
    return (inner + cross).reshape(b, h, l, dh)


def _retention(q, k, v, g, gn_gain):
    b, l, _ = q.shape
    pos = jnp.arange(l, dtype=jnp.float32)
    qh = _rotary(q.astype(jnp.float32).reshape(b, l, RET_HEADS, RET_HEAD_DIM), pos)
    kh = _rotary(k.astype(jnp.float32).reshape(b, l, RET_HEADS, RET_HEAD_DIM), pos) * (RET_HEAD_DIM ** -0.5)
    vh = v.astype(jnp.float32).reshape(b, l, RET_HEADS, RET_HEAD_DIM)
    qh, kh, vh = (t.transpose(0, 2, 1, 3) for t in (qh, kh, vh))
    log_gamma = jnp.log(1.0 - 2.0 ** (-5.0 - jnp.arange(RET_HEADS, dtype=jnp.float32)))
    fwd = _retention_one_direction(qh, kh, vh, log_gamma, False)
    bwd = jnp.flip(_retention_one_direction(jnp.flip(qh, 2), jnp.flip(kh, 2), jnp.flip(vh, 2),
                                            log_gamma, True), 2)
    o = (fwd + bwd).transpose(0, 2, 1, 3)
    o = o * lax.rsqrt(jnp.mean(o * o, axis=-1, keepdims=True) + EPS)
    o = o.reshape(b, l, RET_WIDTH) * gn_gain.astype(jnp.float32)
    return (jax.nn.silu(g.astype(jnp.float32)) * o).astype(q.dtype)


def _complex_scan_combine(e1, e2):
    a1r, a1i, b1r, b1i = e1
    a2r, a2i, b2r, b2i = e2
    ar = a2r * a1r - a2i * a1i
    ai = a2r * a1i + a2i * a1r
    br = a2r * b1r - a2i * b1i + b2r
    bi = a2r * b1i + a2i * b1r + b2i
    return (ar, ai, br, bi)


def _s5(u, lam_re, lam_im, log_dt, b_re, b_im, c_re, c_im, d_skip, glu_w, glu_b):
    bsz, l, _ = u.shape
    uf = u.astype(jnp.float32)
    ug = uf.reshape(bsz, l, SSM_GROUPS, SSM_GROUP)
    y = d_skip.astype(jnp.float32) * uf
    for d in range(2):
        lr = jnp.minimum(lam_re[d].astype(jnp.float32), -1e-4)
        li = lam_im[d].astype(jnp.float32)
        dt = jnp.exp(log_dt[d].astype(jnp.float32))[:, None]
        mag = jnp.exp(lr * dt)
        ar = mag * jnp.cos(li * dt)
        ai = mag * jnp.sin(li * dt)
        den = lr * lr + li * li
        nr = ar - 1.0
        ni = ai
        cr = (nr * lr + ni * li) / den
        ci = (ni * lr - nr * li) / den
        br_ = b_re[d].astype(jnp.float32)
        bi_ = b_im[d].astype(jnp.float32)
        bbr = cr[..., None] * br_ - ci[..., None] * bi_
        bbi = cr[..., None] * bi_ + ci[..., None] * br_
        bu_r = jnp.einsum('blgc,gnc->blgn', ug, bbr)
        bu_i = jnp.einsum('blgc,gnc->blgn', ug, bbi)
        a_r = jnp.broadcast_to(ar, bu_r.shape)
        a_i = jnp.broadcast_to(ai, bu_i.shape)
        _, _, xr, xi = lax.associative_scan(_complex_scan_combine, (a_r, a_i, bu_r, bu_i),
                                            axis=1, reverse=(d == 1))
        yd = (jnp.einsum('blgn,gcn->blgc', xr, c_re[d].astype(jnp.float32))
              - jnp.einsum('blgn,gcn->blgc', xi, c_im[d].astype(jnp.float32)))
        y = y + yd.reshape(bsz, l, SSM_WIDTH)
    y = jax.nn.gelu(y)
    gate = jax.nn.sigmoid(y @ glu_w.astype(jnp.float32) + glu_b.astype(jnp.float32))
    return (y * gate).astype(u.dtype)


def _conv_ffn(h, w_up, conv_w, conv_b, w_down):
    z = h @ w_up
    zp = jnp.pad(z, ((0, 0), (1, 1), (0, 0)))
    z = zp[:, :-2] * conv_w[0] + zp[:, 1:-1] * conv_w[1] + zp[:, 2:] * conv_w[2] + conv_b
    val, gate = jnp.split(z, 2, axis=-1)
    return (jax.nn.gelu(gate) * val) @ w_down


def _trunk(x, norm_mix, w_in, ret_gn_gain, s5_lambda_re, s5_lambda_im, s5_log_dt,
           s5_B_re, s5_B_im, s5_C_re, s5_C_im, s5_D, s5_glu_w, s5_glu_b, w_out,
           norm_ffn, w_up, conv_w, conv_b, w_down, norm_final):
    for i in range(DEPTH):
        h = _rms_norm(x, norm_mix[i])
        proj = h @ w_in[i]
        q, k, v, g, u = jnp.split(proj, [RET_WIDTH, 2 * RET_WIDTH, 3 * RET_WIDTH, 4 * RET_WIDTH], axis=-1)
        ret = _retention(q, k, v, g, ret_gn_gain[i])
        ssm = _s5(u, s5_lambda_re[i], s5_lambda_im[i], s5_log_dt[i], s5_B_re[i], s5_B_im[i],
                  s5_C_re[i], s5_C_im[i], s5_D[i], s5_glu_w[i], s5_glu_b[i])
        x = x + jnp.concatenate([ret, ssm], axis=-1) @ w_out[i]
        h = _rms_norm(x, norm_ffn[i])
        x = x + _conv_ffn(h, w_up[i], conv_w[i], conv_b[i], w_down[i])
    return _rms_norm(x, norm_final)


def setup_inputs(seed: int = 0) -> dict:
    key = jax.random.key(seed)
    ks = jax.random.split(key, 24)
    f32 = jnp.float32
    nrm = lambda k, s, sc: jax.random.normal(k, s, f32) * sc
    n_idx = jnp.arange(SSM_STATE, dtype=f32)
    sdir = (DEPTH, 2, SSM_GROUPS)
    return {
        "x_prompt": jax.random.normal(ks[0], (BATCH, SEQ, D_MODEL), f32),
        "x_sample": jax.random.normal(ks[1], (DEC_BATCH, DEC_SEQ, D_MODEL), f32),
        "norm_mix": 1.0 + nrm(ks[2], (DEPTH, D_MODEL), 0.02),
        "w_in": nrm(ks[3], (DEPTH, D_MODEL, IN_WIDTH), D_MODEL ** -0.5),
        "ret_gn_gain": 1.0 + nrm(ks[4], (DEPTH, RET_WIDTH), 0.02),
        "s5_lambda_re": -0.5 + nrm(ks[5], sdir + (SSM_STATE,), 0.01),
        "s5_lambda_im": math.pi * n_idx + nrm(ks[6], sdir + (SSM_STATE,), 0.01),
        "s5_log_dt": jax.random.uniform(ks[7], sdir, f32, math.log(DT_MIN), math.log(DT_MAX)),
        "s5_B_re": nrm(ks[8], sdir + (SSM_STATE, SSM_GROUP), (2 * SSM_GROUP) ** -0.5),
        "s5_B_im": nrm(ks[9], sdir + (SSM_STATE, SSM_GROUP), (2 * SSM_GROUP) ** -0.5),
        "s5_C_re": nrm(ks[10], sdir + (SSM_GROUP, SSM_STATE), SSM_STATE ** -0.5),
        "s5_C_im": nrm(ks[11], sdir + (SSM_GROUP, SSM_STATE), SSM_STATE ** -0.5),
        "s5_D": nrm(ks[12], (DEPTH, SSM_WIDTH), 1.0),
        "s5_glu_w": nrm(ks[13], (DEPTH, SSM_WIDTH, SSM_WIDTH), SSM_WIDTH ** -0.5),
        "s5_glu_b": nrm(ks[14], (DEPTH, SSM_WIDTH), 0.01),
        "w_out": nrm(ks[15], (DEPTH, MIX_WIDTH, D_MODEL), MIX_WIDTH ** -0.5),
        "norm_ffn": 1.0 + nrm(ks[16], (DEPTH, D_MODEL), 0.02),
        "w_up": nrm(ks[17], (DEPTH, D_MODEL, 2 * D_FF), D_MODEL ** -0.5),
        "conv_w": nrm(ks[18], (DEPTH, CONV_WIDTH, 2 * D_FF), CONV_WIDTH ** -0.5),
        "conv_b": nrm(ks[19], (DEPTH, 2 * D_FF), 0.01),
        "w_down": nrm(ks[20], (DEPTH, D_FF, D_MODEL), D_FF ** -0.5),
        "norm_final": 1.0 + nrm(ks[21], (D_MODEL,), 0.02),
    }


def reference(x_prompt, x_sample, norm_mix, w_in, ret_gn_gain, s5_lambda_re, s5_lambda_im,
              s5_log_dt, s5_B_re, s5_B_im, s5_C_re, s5_C_im, s5_D, s5_glu_w, s5_glu_b,
              w_out, norm_ffn, w_up, conv_w, conv_b, w_down, norm_final):
    y_prompt = _trunk(x_prompt, norm_mix, w_in, ret_gn_gain, s5_lambda_re, s5_lambda_im,
                      s5_log_dt, s5_B_re, s5_B_im, s5_C_re, s5_C_im, s5_D, s5_glu_w, s5_glu_b,
                      w_out, norm_ffn, w_up, conv_w, conv_b, w_down, norm_final)
    y_sample = _trunk(x_sample, norm_mix, w_in, ret_gn_gain, s5_lambda_re, s5_lambda_im,
                      s5_log_dt, s5_B_re, s5_B_im, s5_C_re, s5_C_im, s5_D, s5_glu_w, s5_glu_b,
                      w_out, norm_ffn, w_up, conv_w, conv_b, w_down, norm_final)
    return (y_prompt, y_sample)
```

```python
import functools
import math

import jax
import jax.numpy as jnp
from jax import lax
from jax.experimental import pallas as pl
from jax.experimental.pallas import tpu as pltpu

D_MODEL = 1024
RET_WIDTH = 512
SSM_WIDTH = 512
RET_HEADS = 4
HEAD_DIM = 128
RET_CHUNK = 128
ROPE_BASE = 10000.0
SSM_GROUP = 16
SSM_GROUPS = 32
SSM_STATE = 64
D_FF = 2816
EPS = 1e-6
IN_WIDTH = 4 * RET_WIDTH + SSM_WIDTH

LANES = 128
SUBLANES = 8
BF16_ROWS = 16
VMEM_LIMIT = 56 * 1024 * 1024

SSM_BLOCKS = SSM_WIDTH // LANES
GROUPS_PER_BLOCK = LANES // SSM_GROUP
STATES_PER_BLOCK = GROUPS_PER_BLOCK * SSM_STATE
SSM_CHUNK = 8
TOK_TILE = 512
FF_TILE = 256
HALO = BF16_ROWS

BF16 = jnp.bfloat16
F32 = jnp.float32


def _rms(x, w):
    return x * lax.rsqrt(jnp.mean(x * x, axis=-1, keepdims=True) + EPS) * w


def _const_spec(shape):
    nd = len(shape)
    return pl.BlockSpec(shape, lambda *_: (0,) * nd, pipeline_mode=pl.Buffered(1))


def _in_proj_kernel(x_ref, nw_ref, w_ref, cos_ref, sin_ref, q_ref, k_ref, v_ref, g_ref, u_ref):
    h = _rms(x_ref[...], nw_ref[...]).astype(BF16)
    proj = jnp.dot(h, w_ref[...], preferred_element_type=F32)
    cos = cos_ref[...]
    sin = sin_ref[...]
    kscale = HEAD_DIM ** -0.5
    for hd in range(RET_HEADS):
        lo = hd * HEAD_DIM
        qh = proj[:, lo:lo + HEAD_DIM]
        q_ref[:, lo:lo + HEAD_DIM] = (qh * cos + pltpu.roll(qh, HEAD_DIM // 2, 1) * sin).astype(BF16)
        kh = proj[:, RET_WIDTH + lo:RET_WIDTH + lo + HEAD_DIM]
        kr = kh * cos + pltpu.roll(kh, HEAD_DIM // 2, 1) * sin
        k_ref[:, lo:lo + HEAD_DIM] = (kr * kscale).astype(BF16)
    v_ref[...] = proj[:, 2 * RET_WIDTH:3 * RET_WIDTH].astype(BF16)
    g_ref[...] = proj[:, 3 * RET_WIDTH:4 * RET_WIDTH].astype(BF16)
    for blk in range(SSM_BLOCKS):
        lo = 4 * RET_WIDTH + blk * LANES
        u_ref[blk] = proj[:, lo:lo + LANES].astype(BF16)


def _in_proj(x2, norm_w, w_in, cos2, sin2, seq):
    t = x2.shape[0]
    r = TOK_TILE
    tiles_per_seq = seq // r
    tok = lambda i: (i, 0)
    out_tok = jax.ShapeDtypeStruct((t, RET_WIDTH), BF16)
    return pl.pallas_call(
        _in_proj_kernel,
        grid=(t // r,),
        in_specs=[
            pl.BlockSpec((r, D_MODEL), tok),
            _const_spec((1, D_MODEL)),
            _const_spec((D_MODEL, IN_WIDTH)),
            pl.BlockSpec((r, HEAD_DIM), lambda i: (i % tiles_per_seq, 0)),
            pl.BlockSpec((r, HEAD_DIM), lambda i: (i % tiles_per_seq, 0)),
        ],
        out_specs=[
            pl.BlockSpec((r, RET_WIDTH), tok),
            pl.BlockSpec((r, RET_WIDTH), tok),
            pl.BlockSpec((r, RET_WIDTH), tok),
            pl.BlockSpec((r, RET_WIDTH), tok),
            pl.BlockSpec((SSM_BLOCKS, r, LANES), lambda i: (0, i, 0)),
        ],
        out_shape=[out_tok, out_tok, out_tok, out_tok,
                   jax.ShapeDtypeStruct((SSM_BLOCKS, t, LANES), BF16)],
        compiler_params=pltpu.CompilerParams(
            dimension_semantics=("arbitrary",), vmem_limit_bytes=VMEM_LIMIT),
        name="in_proj",
    )(x2, norm_w, w_in, cos2, sin2)


def _retention_kernel(q_ref, k_ref, v_ref, g_ref, tab_ref, gain_ref, o_ref, kv_scr, r_scr):
    c = RET_CHUNK
    nchunks = q_ref.shape[0] // c
    dmat = tab_ref[0]
    zf = tab_ref[1]
    zb = tab_ref[2]
    xf = tab_ref[3]
    xb = tab_ref[4]
    cd = tab_ref[5]

    def summarize(n, carry):
        rows = pl.ds(pl.multiple_of(n * c, c), c)
        kn = k_ref[rows, :].astype(F32)
        kcat = jnp.concatenate([(kn * zf).astype(BF16), (kn * zb).astype(BF16)], axis=1)
        kv_scr[n] = lax.dot_general(kcat, v_ref[rows, :], (((0,), (0,)), ((), ())),
                                    preferred_element_type=F32)
        return carry

    lax.fori_loop(0, nchunks, summarize, 0)

    rf = jnp.zeros((HEAD_DIM, HEAD_DIM), F32)
    for n in range(nchunks):
        r_scr[n, 0:HEAD_DIM, :] = rf.astype(BF16)
        rf = cd * rf + kv_scr[n, 0:HEAD_DIM, :]
    rb = jnp.zeros((HEAD_DIM, HEAD_DIM), F32)
    for n in reversed(range(nchunks)):
        r_scr[n, HEAD_DIM:2 * HEAD_DIM, :] = rb.astype(BF16)
        rb = cd * rb + kv_scr[n, HEAD_DIM:2 * HEAD_DIM, :]

    gain = gain_ref[...]

    def emit(n, carry):
        rows = pl.ds(pl.multiple_of(n * c, c), c)
        qn = q_ref[rows, :]
        s = lax.dot_general(qn, k_ref[rows, :], (((1,), (1,)), ((), ())),
                            preferred_element_type=F32) * dmat
        inner = jnp.dot(s.astype(BF16), v_ref[rows, :], preferred_element_type=F32)
        qf = qn.astype(F32)
        qcat = jnp.concatenate([(qf * xf).astype(BF16), (qf * xb).astype(BF16)], axis=1)
        o = inner + jnp.dot(qcat, r_scr[n], preferred_element_type=F32)
        o = o * lax.rsqrt(jnp.mean(o * o, axis=-1, keepdims=True) + EPS) * gain
        o_ref[rows, :] = (jax.nn.silu(g_ref[rows, :].astype(F32)) * o).astype(BF16)
        return carry

    lax.fori_loop(0, nchunks, emit, 0)


def _retention(q, k, v, g, tabs, gain, seq):
    t = q.shape[0]
    nchunks = seq // RET_CHUNK
    blk = pl.BlockSpec((seq, HEAD_DIM), lambda b, h: (b, h))
    return pl.pallas_call(
        _retention_kernel,
        grid=(t // seq, RET_HEADS),
        in_specs=[blk, blk, blk, blk,
                  pl.BlockSpec((None, 6, RET_CHUNK, HEAD_DIM), lambda b, h: (h, 0, 0, 0)),
                  pl.BlockSpec((1, HEAD_DIM), lambda b, h: (0, h))],
        out_specs=blk,
        out_shape=jax.ShapeDtypeStruct((t, RET_WIDTH), BF16),
        scratch_shapes=[pltpu.VMEM((nchunks, 2 * HEAD_DIM, HEAD_DIM), F32),
                        pltpu.VMEM((nchunks, 2 * HEAD_DIM, HEAD_DIM), BF16)],
        compiler_params=pltpu.CompilerParams(
            dimension_semantics=("arbitrary", "arbitrary"), vmem_limit_bytes=VMEM_LIMIT),
        name="retention",
    )(q, k, v, g, tabs, gain)


def _retention_tables():
    c = RET_CHUNK
    lg = jnp.log(1.0 - 2.0 ** (-5.0 - jnp.arange(RET_HEADS, dtype=F32)))[:, None, None]
    idx = jnp.arange(c, dtype=F32)
    col = jnp.broadcast_to(idx[:, None], (c, HEAD_DIM))[None]
    dmat = jnp.exp(lg * jnp.abs(idx[:, None] - idx[None, :])[None])
    zf = jnp.exp(lg * (c - 1.0 - col))
    zb = jnp.exp(lg * col)
    xf = jnp.exp(lg * (col + 1.0))
    xb = jnp.exp(lg * (c - col))
    cd = jnp.exp(lg * c) * jnp.ones((1, c, HEAD_DIM), F32)
    return jnp.stack([dmat, zf, zb, xf, xb, cd], axis=1)


def _cmul(ar, ai, br, bi):
    return ar * br - ai * bi, ar * bi + ai * br


def _s5_kernel(u_ref, m_ref, w_ref, v_ref, tab_ref, y_ref, s_scr, x_scr, *, nseq, nrows):
    u = u_ref[...]
    s_scr[...] = jnp.dot(u, w_ref[...], preferred_element_type=F32)
    nblk = nrows // SUBLANES
    sp = STATES_PER_BLOCK
    row = lax.broadcasted_iota(jnp.int32, (SUBLANES, sp), 0)

    def scan_block(base, d, blk, carry):
        rows = pl.ds(pl.multiple_of(base + blk * SUBLANES, SUBLANES), SUBLANES)
        lo = d * 2 * sp
        pr = s_scr[rows, lo:lo + sp]
        pi = s_scr[rows, lo + sp:lo + 2 * sp]
        for step, sh in enumerate((1, 2, 4)):
            shift = sh if d == 0 else SUBLANES - sh
            tr, ti = _cmul(tab_ref[d, step, 0], tab_ref[d, step, 1],
                           pltpu.roll(pr, shift, 0), pltpu.roll(pi, shift, 0))
            pr = pr + tr
            pi = pi + ti
        cr, ci = carry
        tr, ti = _cmul(tab_ref[d, 3, 0], tab_ref[d, 3, 1], cr, ci)
        xr = pr + tr
        xi = pi + ti
        edge = 0 if d == 0 else SUBLANES - 1
        shift = 1 if d == 0 else SUBLANES - 1
        x_scr[rows, lo:lo + sp] = jnp.where(row == edge, cr, pltpu.roll(xr, shift, 0))
        x_scr[rows, lo + sp:lo + 2 * sp] = jnp.where(row == edge, ci, pltpu.roll(xi, shift, 0))
        last = SUBLANES - 1 if d == 0 else 0
        return (jnp.broadcast_to(xr[last:last + 1, :], (SUBLANES, sp)),
                jnp.broadcast_to(xi[last:last + 1, :], (SUBLANES, sp)))

    zero = jnp.zeros((SUBLANES, sp), F32)
    for sq in range(nseq):
        base = sq * nrows

        def body(i, carry, base=base):
            cf = scan_block(base, 0, i, carry[0:2])
            cb = scan_block(base, 1, nblk - 1 - i, carry[2:4])
            return cf + cb

        lax.fori_loop(0, nblk, body, (zero, zero, zero, zero))

    y = jnp.dot(u, m_ref[...], preferred_element_type=F32)
    y = y + jnp.dot(x_scr[...].astype(BF16), v_ref[...], preferred_element_type=F32)
    y_ref[...] = y.astype(BF16)


def _s5(u_blocks, mats, seq):
    m_mat, w_mat, v_mat, tabs = mats
    nb, t, _ = u_blocks.shape
    c = SSM_CHUNK
    nrows = seq // c
    nseq = 2
    rows = nseq * nrows
    width = c * LANES
    u_rows = u_blocks.reshape(nb, t // c, width)
    y_rows = pl.pallas_call(
        functools.partial(_s5_kernel, nseq=nseq, nrows=nrows),
        grid=(nb, t // (c * rows)),
        in_specs=[
            pl.BlockSpec((None, rows, width), lambda b, i: (b, i, 0)),
            pl.BlockSpec((None, width, width), lambda b, i: (b, 0, 0)),
            pl.BlockSpec((None, width, 4 * STATES_PER_BLOCK), lambda b, i: (b, 0, 0)),
            pl.BlockSpec((None, 4 * STATES_PER_BLOCK, width), lambda b, i: (b, 0, 0)),
            pl.BlockSpec((None, 2, 4, 2, SUBLANES, STATES_PER_BLOCK),
                         lambda b, i: (b, 0, 0, 0, 0, 0)),
        ],
        out_specs=pl.BlockSpec((None, rows, width), lambda b, i: (b, i, 0)),
        out_shape=jax.ShapeDtypeStruct((nb, t // c, width), BF16),
        scratch_shapes=[pltpu.VMEM((rows, 4 * STATES_PER_BLOCK), F32),
                        pltpu.VMEM((rows, 4 * STATES_PER_BLOCK), F32)],
        compiler_params=pltpu.CompilerParams(
            dimension_semantics=("arbitrary", "arbitrary"), vmem_limit_bytes=VMEM_LIMIT),
        name="s5",
    )(u_rows, m_mat, w_mat, v_mat, tabs)
    return y_rows.reshape(nb, t, LANES).transpose(1, 0, 2).reshape(t, SSM_WIDTH)


def _s5_matrices(lam_re, lam_im, log_dt, b_re, b_im, c_re, c_im, d_skip):
    c = SSM_CHUNK
    hp = lax.Precision.HIGHEST
    lr = jnp.minimum(lam_re.astype(F32), -1e-4)
    li = lam_im.astype(F32)
    dt = jnp.exp(log_dt.astype(F32))[..., None]
    mag = jnp.exp(lr * dt)
    ar = mag * jnp.cos(li * dt)
    ai = mag * jnp.sin(li * dt)
    den = lr * lr + li * li
    nr = ar - 1.0
    ni = ai
    cr = (nr * lr + ni * li) / den
    ci = (ni * lr - nr * li) / den
    bbr = cr[..., None] * b_re - ci[..., None] * b_im
    bbi = cr[..., None] * b_im + ci[..., None] * b_re

    pr = [jnp.ones_like(ar)]
    pi = [jnp.zeros_like(ai)]
    for _ in range(SUBLANES * c):
        nxt = _cmul(pr[-1], pi[-1], ar, ai)
        pr.append(nxt[0])
        pi.append(nxt[1])
    pr = jnp.stack(pr)
    pi = jnp.stack(pi)

    er = c_re[None] * pr[:c, :, :, None, :] - c_im[None] * pi[:c, :, :, None, :]
    ei = c_re[None] * pi[:c, :, :, None, :] + c_im[None] * pr[:c, :, :, None, :]
    kern = (jnp.einsum("pdgon,dgni->pdgio", er, bbr, precision=hp)
            - jnp.einsum("pdgon,dgni->pdgio", ei, bbi, precision=hp))
    eye_ch = jnp.eye(SSM_GROUP, dtype=F32)
    diag = kern[0, 0] + kern[0, 1] + d_skip.reshape(SSM_GROUPS, SSM_GROUP)[:, :, None] * eye_ch
    tok = jnp.arange(c)
    lag = tok[None, :] - tok[:, None]
    ksel = jnp.where((lag > 0)[..., None, None, None], kern[jnp.abs(lag), 0],
                     jnp.where((lag < 0)[..., None, None, None], kern[jnp.abs(lag), 1],
                               diag[None, None]))
    eye_g = jnp.eye(GROUPS_PER_BLOCK, dtype=F32)
    ksel = ksel.reshape(c, c, SSM_BLOCKS, GROUPS_PER_BLOCK, SSM_GROUP, SSM_GROUP)
    m_mat = jnp.einsum("stbgio,gh->bsgitho", ksel, eye_g).reshape(
        SSM_BLOCKS, c * LANES, c * LANES)

    pw = jnp.stack([pr[c - 1 - tok, 0], pr[tok, 1]], axis=1)
    pwi = jnp.stack([pi[c - 1 - tok, 0], pi[tok, 1]], axis=1)
    wr = pw[..., None] * bbr[None] - pwi[..., None] * bbi[None]
    wi = pw[..., None] * bbi[None] + pwi[..., None] * bbr[None]
    wcat = jnp.stack([wr, wi], axis=2)
    wcat = wcat.reshape(c, 2, 2, SSM_BLOCKS, GROUPS_PER_BLOCK, SSM_STATE, SSM_GROUP)
    w_mat = jnp.einsum("sdpbgni,gh->bsgidphn", wcat, eye_g).reshape(
        SSM_BLOCKS, c * LANES, 4 * STATES_PER_BLOCK)

    qr = jnp.stack([pr[tok + 1, 0], pr[c - tok, 1]], axis=1)
    qi = jnp.stack([pi[tok + 1, 0], pi[c - tok, 1]], axis=1)
    vr = c_re[None] * qr[:, :, :, None, :] - c_im[None] * qi[:, :, :, None, :]
    vi = -(c_re[None] * qi[:, :, :, None, :] + c_im[None] * qr[:, :, :, None, :])
    vcat = jnp.stack([vr, vi], axis=2)
    vcat = vcat.reshape(c, 2, 2, SSM_BLOCKS, GROUPS_PER_BLOCK, SSM_GROUP, SSM_STATE)
    v_mat = jnp.einsum("tdpbgon,gh->bdpgntho", vcat, eye_g).reshape(
        SSM_BLOCKS, 4 * STATES_PER_BLOCK, c * LANES)

    rowi = jnp.arange(SUBLANES)

    def lanes(x):
        return x.reshape(x.shape[:-2] + (SSM_BLOCKS, STATES_PER_BLOCK))

    tabs = []
    for d in range(2):
        per_dir = []
        for sh in (1, 2, 4):
            keep = (rowi >= sh) if d == 0 else (rowi < SUBLANES - sh)
            per_dir.append(jnp.stack([
                jnp.where(keep[:, None, None], lanes(pr[sh * c, d])[None], 0.0),
                jnp.where(keep[:, None, None], lanes(pi[sh * c, d])[None], 0.0)]))
        cpow = (rowi + 1) * c if d == 0 else (SUBLANES - rowi) * c
        per_dir.append(jnp.stack([lanes(pr[cpow, d]), lanes(pi[cpow, d])]))
        tabs.append(jnp.stack(per_dir))
    tabs = jnp.stack(tabs).transpose(4, 0, 1, 2, 3, 5)
    return m_mat.astype(BF16), w_mat.astype(BF16), v_mat.astype(BF16), tabs


def _out_ffn_kernel(xm_ref, xp_ref, xn_ref, rm_ref, rp_ref, rn_ref, sm_ref, sp_ref, sn_ref,
                    glu_w_ref, glu_b_ref, w_out_ref, nffn_ref, w_up_ref, cw_ref, cb_ref,
                    w_down_ref, nfin_ref, o_ref, hid_scr, *, tiles_per_seq):
    r = xm_ref.shape[0]
    x = jnp.concatenate([xp_ref[...], xm_ref[...], xn_ref[...]], axis=0)
    ret = jnp.concatenate([rp_ref[...], rm_ref[...], rn_ref[...]], axis=0)
    ssm = jnp.concatenate([sp_ref[...], sm_ref[...], sn_ref[...]], axis=0).astype(F32)
    y = jax.nn.gelu(ssm)
    gate = jax.nn.sigmoid(
        jnp.dot(y.astype(BF16), glu_w_ref[...], preferred_element_type=F32) + glu_b_ref[...])
    mix = jnp.concatenate([ret, (y * gate).astype(BF16)], axis=1)
    x1 = x + jnp.dot(mix, w_out_ref[...], preferred_element_type=F32)
    h = _rms(x1, nffn_ref[...])
    j = pl.program_id(0) % tiles_per_seq
    ext = r + 2 * HALO
    rowid = lax.broadcasted_iota(jnp.int32, (ext, 1), 0)
    first_live = jnp.where(j == 0, HALO, 0)
    end_live = jnp.where(j == tiles_per_seq - 1, r + HALO, ext)
    h = jnp.where((rowid >= first_live) & (rowid < end_live), h, 0.0).astype(BF16)
    for jj in range(D_FF // FF_TILE):
        parts = []
        for off in (jj * FF_TILE, D_FF + jj * FF_TILE):
            z = jnp.dot(h, w_up_ref[:, off:off + FF_TILE], preferred_element_type=F32)
            zc = (pltpu.roll(z, 1, 0)[HALO:HALO + r] * cw_ref[0:1, off:off + FF_TILE]
                  + z[HALO:HALO + r] * cw_ref[1:2, off:off + FF_TILE]
                  + pltpu.roll(z, ext - 1, 0)[HALO:HALO + r] * cw_ref[2:3, off:off + FF_TILE]
                  + cb_ref[:, off:off + FF_TILE])
            parts.append(zc)
        hid_scr[:, jj * FF_TILE:(jj + 1) * FF_TILE] = (jax.nn.gelu(parts[1]) * parts[0]).astype(BF16)
    x2 = x1[HALO:HALO + r] + jnp.dot(hid_scr[...], w_down_ref[...], preferred_element_type=F32)
    o_ref[...] = _rms(x2, nfin_ref[...])


def _out_ffn(x2, ret, ssm, glu_w, glu_b, w_out, norm_ffn, w_up, conv_w, conv_b, w_down,
             norm_final, seq):
    t = x2.shape[0]
    r = TOK_TILE
    tiles_per_seq = seq // r
    hb = r // HALO
    last = t // HALO - 1
    main = lambda i: (i, 0)
    prev = lambda i: (jnp.maximum(i * hb - 1, 0), 0)
    nxt = lambda i: (jnp.minimum((i + 1) * hb, last), 0)

    def trio(width):
        return [pl.BlockSpec((r, width), main), pl.BlockSpec((HALO, width), prev),
                pl.BlockSpec((HALO, width), nxt)]

    return pl.pallas_call(
        functools.partial(_out_ffn_kernel, tiles_per_seq=tiles_per_seq),
        grid=(t // r,),
        in_specs=trio(D_MODEL) + trio(RET_WIDTH) + trio(SSM_WIDTH) + [
            _const_spec((SSM_WIDTH, SSM_WIDTH)), _const_spec((1, SSM_WIDTH)),
            _const_spec((D_MODEL, D_MODEL)), _const_spec((1, D_MODEL)),
            _const_spec((D_MODEL, 2 * D_FF)), _const_spec((3, 2 * D_FF)),
            _const_spec((1, 2 * D_FF)), _const_spec((D_FF, D_MODEL)),
            _const_spec((1, D_MODEL))],
        out_specs=pl.BlockSpec((r, D_MODEL), main),
        out_shape=jax.ShapeDtypeStruct((t, D_MODEL), F32),
        scratch_shapes=[pltpu.VMEM((r, D_FF), BF16)],
        compiler_params=pltpu.CompilerParams(
            dimension_semantics=("arbitrary",), vmem_limit_bytes=VMEM_LIMIT),
        name="out_ffn",
    )(x2, x2, x2, ret, ret, ret, ssm, ssm, ssm, glu_w, glu_b, w_out, norm_ffn, w_up, conv_w,
      conv_b, w_down, norm_final)


def _rotary_tables(seq):
    half = HEAD_DIM // 2
    inv_freq = ROPE_BASE ** (-jnp.arange(half, dtype=F32) / half)
    ang = jnp.arange(seq, dtype=F32)[:, None] * inv_freq[None, :]
    cos = jnp.cos(ang)
    sin = jnp.sin(ang)
    return jnp.concatenate([cos, cos], axis=1), jnp.concatenate([-sin, sin], axis=1)


def _layer(x, prep):
    b, seq, _ = x.shape
    x2 = x.reshape(b * seq, D_MODEL)
    q, k, v, g, u = _in_proj(x2, prep["norm_mix"], prep["w_in"], prep["cos"], prep["sin"], seq)
    ret = _retention(q, k, v, g, prep["ret_tabs"], prep["gn_gain"], seq)
    ssm = _s5(u, prep["s5"], seq)
    out = _out_ffn(x2, ret, ssm, prep["glu_w"], prep["glu_b"], prep["w_out"], prep["norm_ffn"],
                   prep["w_up"], prep["conv_w"], prep["conv_b"], prep["w_down"],
                   prep["norm_final"], seq)
    return out.reshape(b, seq, D_MODEL)


def kernel(x_prompt, x_sample, norm_mix, w_in, ret_gn_gain, s5_lambda_re, s5_lambda_im, s5_log_dt, s5_B_re, s5_B_im, s5_C_re, s5_C_im, s5_D, s5_glu_w, s5_glu_b, w_out, norm_ffn, w_up, conv_w, conv_b, w_down, norm_final):
    assert norm_mix.shape[0] == 1, "single-layer trunk"
    seq = x_prompt.shape[1]
    assert x_sample.shape[1] == seq
    cos, sin = _rotary_tables(seq)
    prep = {
        "norm_mix": norm_mix[0][None].astype(F32),
        "w_in": w_in[0].astype(BF16),
        "cos": cos, "sin": sin,
        "ret_tabs": _retention_tables(),
        "gn_gain": ret_gn_gain[0][None].astype(F32),
        "s5": _s5_matrices(s5_lambda_re[0], s5_lambda_im[0], s5_log_dt[0], s5_B_re[0], s5_B_im[0],
                           s5_C_re[0], s5_C_im[0], s5_D[0]),
        "glu_w": s5_glu_w[0].astype(BF16),
        "glu_b": s5_glu_b[0][None].astype(F32),
        "w_out": w_out[0].astype(BF16),
        "norm_ffn": norm_ffn[0][None].astype(F32),
        "w_up": w_up[0].astype(BF16),
        "conv_w": conv_w[0].astype(F32),
        "conv_b": conv_b[0][None].astype(F32),
        "w_down": w_down[0].astype(BF16),
        "norm_final": norm_final[None].astype(F32),
    }
    return (_layer(x_prompt, prep), _layer(x_sample, prep))
```

```python
import functools

import jax
import jax.numpy as jnp
from jax import lax
from jax.experimental import pallas as pl
from jax.experimental.pallas import tpu as pltpu

D_MODEL = 1024
RET_WIDTH = 512
SSM_WIDTH = 512
RET_HEADS = 4
HEAD_DIM = 128
RET_CHUNK = 128
ROPE_BASE = 10000.0
SSM_GROUP = 16
SSM_GROUPS = 32
SSM_STATE = 64
D_FF = 2816
EPS = 1e-6
IN_WIDTH = 4 * RET_WIDTH + SSM_WIDTH

LANES = 128
SUBLANES = 8
BF16_ROWS = 16
VMEM_LIMIT = 56 * 1024 * 1024

SSM_BLOCKS = SSM_WIDTH // LANES
GROUPS_PER_BLOCK = LANES // SSM_GROUP
STATES_PER_BLOCK = GROUPS_PER_BLOCK * SSM_STATE
SSM_CHUNK = 8
SSM_SEQS = 2
TOK_TILE = 512
FF_TILE = 256
HALO = BF16_ROWS

BF16 = jnp.bfloat16
F32 = jnp.float32


def _rms(x, w):
    return x * lax.rsqrt(jnp.mean(x * x, axis=-1, keepdims=True) + EPS) * w


def _const_spec(shape):
    nd = len(shape)
    return pl.BlockSpec(shape, lambda *_: (0,) * nd, pipeline_mode=pl.Buffered(1))


def _in_proj_kernel(x_ref, nw_ref, w_ref, cos_ref, sin_ref, q_ref, k_ref, v_ref, g_ref, u_ref):
    h = _rms(x_ref[...], nw_ref[...]).astype(BF16)
    proj = jnp.dot(h, w_ref[...], preferred_element_type=F32)
    cos = cos_ref[...]
    sin = sin_ref[...]
    kscale = HEAD_DIM ** -0.5
    for hd in range(RET_HEADS):
        lo = hd * HEAD_DIM
        qh = proj[:, lo:lo + HEAD_DIM]
        q_ref[:, lo:lo + HEAD_DIM] = (qh * cos + pltpu.roll(qh, HEAD_DIM // 2, 1) * sin).astype(BF16)
        kh = proj[:, RET_WIDTH + lo:RET_WIDTH + lo + HEAD_DIM]
        kr = kh * cos + pltpu.roll(kh, HEAD_DIM // 2, 1) * sin
        k_ref[:, lo:lo + HEAD_DIM] = (kr * kscale).astype(BF16)
    v_ref[...] = proj[:, 2 * RET_WIDTH:3 * RET_WIDTH].astype(BF16)
    g_ref[...] = proj[:, 3 * RET_WIDTH:4 * RET_WIDTH].astype(BF16)
    for blk in range(SSM_BLOCKS):
        lo = 4 * RET_WIDTH + blk * LANES
        u_ref[blk] = proj[:, lo:lo + LANES]


def _in_proj(x2, norm_w, w_in, cos2, sin2, seq):
    t = x2.shape[0]
    r = TOK_TILE
    tiles_per_seq = seq // r
    tok = lambda i: (i, 0)
    out_tok = jax.ShapeDtypeStruct((t, RET_WIDTH), BF16)
    return pl.pallas_call(
        _in_proj_kernel,
        grid=(t // r,),
        in_specs=[
            pl.BlockSpec((r, D_MODEL), tok),
            _const_spec((1, D_MODEL)),
            _const_spec((D_MODEL, IN_WIDTH)),
            pl.BlockSpec((r, HEAD_DIM), lambda i: (i % tiles_per_seq, 0)),
            pl.BlockSpec((r, HEAD_DIM), lambda i: (i % tiles_per_seq, 0)),
        ],
        out_specs=[
            pl.BlockSpec((r, RET_WIDTH), tok),
            pl.BlockSpec((r, RET_WIDTH), tok),
            pl.BlockSpec((r, RET_WIDTH), tok),
            pl.BlockSpec((r, RET_WIDTH), tok),
            pl.BlockSpec((SSM_BLOCKS, r, LANES), lambda i: (0, i, 0)),
        ],
        out_shape=[out_tok, out_tok, out_tok, out_tok,
                   jax.ShapeDtypeStruct((SSM_BLOCKS, t, LANES), F32)],
        compiler_params=pltpu.CompilerParams(
            dimension_semantics=("arbitrary",), vmem_limit_bytes=VMEM_LIMIT),
        name="in_proj",
    )(x2, norm_w, w_in, cos2, sin2)


def _retention_kernel(q_ref, k_ref, v_ref, g_ref, tab_ref, gain_ref, o_ref, kv_scr, r_scr):
    c = RET_CHUNK
    nchunks = q_ref.shape[0] // c
    heads = [slice(hd * HEAD_DIM, (hd + 1) * HEAD_DIM) for hd in range(RET_HEADS)]

    def summarize(n, carry):
        rows = pl.ds(pl.multiple_of(n * c, c), c)
        for hd, cols in enumerate(heads):
            kn = k_ref[rows, cols].astype(F32)
            kcat = jnp.concatenate([(kn * tab_ref[hd, 1]).astype(BF16),
                                    (kn * tab_ref[hd, 2]).astype(BF16)], axis=1)
            kv_scr[n, hd] = lax.dot_general(kcat, v_ref[rows, cols], (((0,), (0,)), ((), ())),
                                            preferred_element_type=F32)
        return carry

    lax.fori_loop(0, nchunks, summarize, 0, unroll=2)

    for hd in range(RET_HEADS):
        cd = tab_ref[hd, 5]
        rf = jnp.zeros((HEAD_DIM, HEAD_DIM), F32)
        for n in range(nchunks):
            r_scr[n, hd, 0:HEAD_DIM, :] = rf.astype(BF16)
            rf = cd * rf + kv_scr[n, hd, 0:HEAD_DIM, :]
        rb = jnp.zeros((HEAD_DIM, HEAD_DIM), F32)
        for n in reversed(range(nchunks)):
            r_scr[n, hd, HEAD_DIM:2 * HEAD_DIM, :] = rb.astype(BF16)
            rb = cd * rb + kv_scr[n, hd, HEAD_DIM:2 * HEAD_DIM, :]

    def emit(n, carry):
        rows = pl.ds(pl.multiple_of(n * c, c), c)
        for hd, cols in enumerate(heads):
            qn = q_ref[rows, cols]
            s = lax.dot_general(qn, k_ref[rows, cols], (((1,), (1,)), ((), ())),
                                preferred_element_type=F32) * tab_ref[hd, 0]
            inner = jnp.dot(s.astype(BF16), v_ref[rows, cols], preferred_element_type=F32)
            qf = qn.astype(F32)
            qcat = jnp.concatenate([(qf * tab_ref[hd, 3]).astype(BF16),
                                    (qf * tab_ref[hd, 4]).astype(BF16)], axis=1)
            o = inner + jnp.dot(qcat, r_scr[n, hd], preferred_element_type=F32)
            o = o * lax.rsqrt(jnp.mean(o * o, axis=-1, keepdims=True) + EPS) * gain_ref[:, cols]
            o_ref[rows, cols] = (jax.nn.silu(g_ref[rows, cols].astype(F32)) * o).astype(BF16)
        return carry

    lax.fori_loop(0, nchunks, emit, 0, unroll=2)


def _retention(q, k, v, g, tabs, gain, seq):
    t = q.shape[0]
    nchunks = seq // RET_CHUNK
    blk = pl.BlockSpec((seq, RET_WIDTH), lambda b: (b, 0))
    return pl.pallas_call(
        _retention_kernel,
        grid=(t // seq,),
        in_specs=[blk, blk, blk, blk,
                  _const_spec((RET_HEADS, 6, RET_CHUNK, HEAD_DIM)),
                  _const_spec((1, RET_WIDTH))],
        out_specs=blk,
        out_shape=jax.ShapeDtypeStruct((t, RET_WIDTH), BF16),
        scratch_shapes=[pltpu.VMEM((nchunks, RET_HEADS, 2 * HEAD_DIM, HEAD_DIM), F32),
                        pltpu.VMEM((nchunks, RET_HEADS, 2 * HEAD_DIM, HEAD_DIM), BF16)],
        compiler_params=pltpu.CompilerParams(
            dimension_semantics=("arbitrary",), vmem_limit_bytes=VMEM_LIMIT),
        name="retention",
    )(q, k, v, g, tabs, gain)


def _retention_tables():
    c = RET_CHUNK
    lg = jnp.log(1.0 - 2.0 ** (-5.0 - jnp.arange(RET_HEADS, dtype=F32)))[:, None, None]
    idx = jnp.arange(c, dtype=F32)
    col = jnp.broadcast_to(idx[:, None], (c, HEAD_DIM))[None]
    dmat = jnp.exp(lg * jnp.abs(idx[:, None] - idx[None, :])[None])
    zf = jnp.exp(lg * (c - 1.0 - col))
    zb = jnp.exp(lg * col)
    xf = jnp.exp(lg * (col + 1.0))
    xb = jnp.exp(lg * (c - col))
    cd = jnp.exp(lg * c) * jnp.ones((1, c, HEAD_DIM), F32)
    return jnp.stack([dmat, zf, zb, xf, xb, cd], axis=1)


def _cmul(ar, ai, br, bi):
    return ar * br - ai * bi, ar * bi + ai * br


def _lane_group(shape, width):
    lane = lax.broadcasted_iota(jnp.int32, shape, 1)
    return (lane >> (width.bit_length() - 1)) & (GROUPS_PER_BLOCK - 1)


def _s5_kernel(u_ref, mc_ref, wc_ref, vc_ref, tab_ref, y_ref,
               m_scr, w_scr, v_scr, s_scr, x_scr, *, nseq, nrows):
    c = SSM_CHUNK
    sp = STATES_PER_BLOCK
    rows_all = nseq * nrows

    @pl.when(pl.program_id(1) == 0)
    def _expand():
        grp_m = _lane_group((SSM_GROUP, c * LANES), SSM_GROUP)
        grp_w = _lane_group((SSM_GROUP, 4 * sp), SSM_STATE)
        grp_v = _lane_group((SSM_STATE, c * LANES), SSM_GROUP)
        for g in range(GROUPS_PER_BLOCK):
            for s in range(c):
                src = slice(s * SSM_GROUP, (s + 1) * SSM_GROUP)
                dst = slice(s * LANES + g * SSM_GROUP, s * LANES + (g + 1) * SSM_GROUP)
                m_scr[dst, :] = jnp.where(grp_m == g, mc_ref[src, :], 0)
                w_scr[dst, :] = jnp.where(grp_w == g, wc_ref[src, :], 0)
            for dp in range(4):
                dst = slice(dp * sp + g * SSM_STATE, dp * sp + (g + 1) * SSM_STATE)
                v_scr[dst, :] = jnp.where(
                    grp_v == g, vc_ref[dp * SSM_STATE:(dp + 1) * SSM_STATE, :], 0)

    u = jnp.concatenate(
        [u_ref[pl.ds(s, rows_all, stride=c), :].astype(BF16) for s in range(c)], axis=1)
    s_scr[...] = jnp.dot(u, w_scr[...], preferred_element_type=F32)
    nblk = nrows // SUBLANES
    row = lax.broadcasted_iota(jnp.int32, (SUBLANES, sp), 0)

    def scan_block(base, d, blk, carry):
        rows = pl.ds(pl.multiple_of(base + blk * SUBLANES, SUBLANES), SUBLANES)
        lo = d * 2 * sp
        pr = s_scr[rows, lo:lo + sp]
        pi = s_scr[rows, lo + sp:lo + 2 * sp]
        for step, sh in enumerate((1, 2, 4)):
            shift = sh if d == 0 else SUBLANES - sh
            tr, ti = _cmul(tab_ref[d, step, 0], tab_ref[d, step, 1],
                           pltpu.roll(pr, shift, 0), pltpu.roll(pi, shift, 0))
            pr = pr + tr
            pi = pi + ti
        cr, ci = carry
        tr, ti = _cmul(tab_ref[d, 3, 0], tab_ref[d, 3, 1], cr, ci)
        xr = pr + tr
        xi = pi + ti
        edge = 0 if d == 0 else SUBLANES - 1
        shift = 1 if d == 0 else SUBLANES - 1
        x_scr[rows, lo:lo + sp] = jnp.where(row == edge, cr, pltpu.roll(xr, shift, 0))
        x_scr[rows, lo + sp:lo + 2 * sp] = jnp.where(row == edge, ci, pltpu.roll(xi, shift, 0))
        last = SUBLANES - 1 if d == 0 else 0
        return (jnp.broadcast_to(xr[last:last + 1, :], (SUBLANES, sp)),
                jnp.broadcast_to(xi[last:last + 1, :], (SUBLANES, sp)))

    zero = jnp.zeros((SUBLANES, sp), F32)

    def body(i, carry):
        out = ()
        for sq in range(nseq):
            cf = scan_block(sq * nrows, 0, i, carry[4 * sq:4 * sq + 2])
            cb = scan_block(sq * nrows, 1, nblk - 1 - i, carry[4 * sq + 2:4 * sq + 4])
            out = out + cf + cb
        return out

    lax.fori_loop(0, nblk, body, (zero,) * (4 * nseq))

    y = jnp.dot(u, m_scr[...], preferred_element_type=F32)
    y = y + jnp.dot(x_scr[...].astype(BF16), v_scr[...], preferred_element_type=F32)
    for s in range(c):
        y_ref[pl.ds(s, rows_all, stride=c), :] = y[:, s * LANES:(s + 1) * LANES]


def _s5(u_blocks, mats, seq):
    mc, wc, vc, tabs = mats
    nb, t, _ = u_blocks.shape
    c = SSM_CHUNK
    nrows = seq // c
    rows = SSM_SEQS * nrows
    width = c * LANES
    sp = STATES_PER_BLOCK
    return pl.pallas_call(
        functools.partial(_s5_kernel, nseq=SSM_SEQS, nrows=nrows),
        grid=(nb, t // (c * rows)),
        in_specs=[
            pl.BlockSpec((None, rows * c, LANES), lambda b, i: (b, i, 0)),
            pl.BlockSpec((None, c * SSM_GROUP, width), lambda b, i: (b, 0, 0)),
            pl.BlockSpec((None, c * SSM_GROUP, 4 * sp), lambda b, i: (b, 0, 0)),
            pl.BlockSpec((None, 4 * SSM_STATE, width), lambda b, i: (b, 0, 0)),
            pl.BlockSpec((None, 2, 4, 2, SUBLANES, sp), lambda b, i: (b, 0, 0, 0, 0, 0)),
        ],
        out_specs=pl.BlockSpec((None, rows * c, LANES), lambda b, i: (b, i, 0)),
        out_shape=jax.ShapeDtypeStruct((nb, t, LANES), F32),
        scratch_shapes=[pltpu.VMEM((width, width), BF16),
                        pltpu.VMEM((width, 4 * sp), BF16),
                        pltpu.VMEM((4 * sp, width), BF16),
                        pltpu.VMEM((rows, 4 * sp), F32),
                        pltpu.VMEM((rows, 4 * sp), F32)],
        compiler_params=pltpu.CompilerParams(
            dimension_semantics=("arbitrary", "arbitrary"), vmem_limit_bytes=VMEM_LIMIT),
        name="s5",
    )(u_blocks, mc, wc, vc, tabs)


def _s5_matrices(lam_re, lam_im, log_dt, b_re, b_im, c_re, c_im, d_skip):
    c = SSM_CHUNK
    nb, gpb, ch, ns = SSM_BLOCKS, GROUPS_PER_BLOCK, SSM_GROUP, SSM_STATE
    hp = lax.Precision.HIGHEST
    lr = jnp.minimum(lam_re.astype(F32), -1e-4)
    li = lam_im.astype(F32)
    dt = jnp.exp(log_dt.astype(F32))[..., None]
    p = jnp.arange(SUBLANES * c + 1, dtype=F32)[:, None, None, None]
    pmag = jnp.exp(p * (lr * dt)[None])
    pr = pmag * jnp.cos(p * (li * dt)[None])
    pi = pmag * jnp.sin(p * (li * dt)[None])
    ar, ai = pr[1], pi[1]
    den = lr * lr + li * li
    nr = ar - 1.0
    ni = ai
    cr = (nr * lr + ni * li) / den
    ci = (ni * lr - nr * li) / den
    bbr = cr[..., None] * b_re - ci[..., None] * b_im
    bbi = cr[..., None] * b_im + ci[..., None] * b_re

    er = c_re[None] * pr[:c, :, :, None, :] - c_im[None] * pi[:c, :, :, None, :]
    ei = c_re[None] * pi[:c, :, :, None, :] + c_im[None] * pr[:c, :, :, None, :]
    kern = (jnp.einsum("pdgon,dgni->pdgio", er, bbr, precision=hp)
            - jnp.einsum("pdgon,dgni->pdgio", ei, bbi, precision=hp))
    eye_ch = jnp.eye(ch, dtype=F32)
    diag = kern[0, 0] + kern[0, 1] + d_skip.reshape(SSM_GROUPS, ch)[:, :, None] * eye_ch
    tok = jnp.arange(c)
    lag = tok[None, :] - tok[:, None]
    ksel = jnp.where((lag > 0)[..., None, None, None], kern[jnp.abs(lag), 0],
                     jnp.where((lag < 0)[..., None, None, None], kern[jnp.abs(lag), 1],
                               diag[None, None]))
    mc = ksel.reshape(c, c, nb, gpb, ch, ch).transpose(2, 0, 4, 1, 3, 5).reshape(
        nb, c * ch, c * LANES)

    pw = jnp.stack([pr[c - 1 - tok, 0], pr[tok, 1]], axis=1)
    pwi = jnp.stack([pi[c - 1 - tok, 0], pi[tok, 1]], axis=1)
    wr = pw[..., None] * bbr[None] - pwi[..., None] * bbi[None]
    wi = pw[..., None] * bbi[None] + pwi[..., None] * bbr[None]
    wcat = jnp.stack([wr, wi], axis=2)
    wc = wcat.reshape(c, 2, 2, nb, gpb, ns, ch).transpose(3, 0, 6, 1, 2, 4, 5).reshape(
        nb, c * ch, 4 * STATES_PER_BLOCK)

    qr = jnp.stack([pr[tok + 1, 0], pr[c - tok, 1]], axis=1)
    qi = jnp.stack([pi[tok + 1, 0], pi[c - tok, 1]], axis=1)
    vr = c_re[None] * qr[:, :, :, None, :] - c_im[None] * qi[:, :, :, None, :]
    vi = -(c_re[None] * qi[:, :, :, None, :] + c_im[None] * qr[:, :, :, None, :])
    vcat = jnp.stack([vr, vi], axis=2)
    vc = vcat.reshape(c, 2, 2, nb, gpb, ch, ns).transpose(3, 1, 2, 6, 0, 4, 5).reshape(
        nb, 4 * ns, c * LANES)

    rowi = jnp.arange(SUBLANES)

    def lanes(x):
        return x.reshape(x.shape[:-2] + (nb, STATES_PER_BLOCK))

    tabs = []
    for d in range(2):
        per_dir = []
        for sh in (1, 2, 4):
            keep = (rowi >= sh) if d == 0 else (rowi < SUBLANES - sh)
            per_dir.append(jnp.stack([
                jnp.where(keep[:, None, None], lanes(pr[sh * c, d])[None], 0.0),
                jnp.where(keep[:, None, None], lanes(pi[sh * c, d])[None], 0.0)]))
        cpow = (rowi + 1) * c if d == 0 else (SUBLANES - rowi) * c
        per_dir.append(jnp.stack([lanes(pr[cpow, d]), lanes(pi[cpow, d])]))
        tabs.append(jnp.stack(per_dir))
    tabs = jnp.stack(tabs).transpose(4, 0, 1, 2, 3, 5)
    return mc.astype(BF16), wc.astype(BF16), vc.astype(BF16), tabs


def _out_ffn_kernel(xm_ref, xp_ref, xn_ref, rm_ref, rp_ref, rn_ref, sm_ref, sp_ref, sn_ref,
                    glu_w_ref, glu_b_ref, w_out_ref, nffn_ref, w_up_ref, cw_ref, cb_ref,
                    w_down_ref, nfin_ref, o_ref, hid_scr, *, tiles_per_seq):
    r = xm_ref.shape[0]
    x = jnp.concatenate([xp_ref[...], xm_ref[...], xn_ref[...]], axis=0)
    ret = jnp.concatenate([rp_ref[...], rm_ref[...], rn_ref[...]], axis=0)
    ssm = jnp.concatenate(
        [jnp.concatenate([sp_ref[b], sm_ref[b], sn_ref[b]], axis=0) for b in range(SSM_BLOCKS)],
        axis=1)
    y = jax.nn.gelu(ssm)
    gate = jax.nn.sigmoid(
        jnp.dot(y.astype(BF16), glu_w_ref[...], preferred_element_type=F32) + glu_b_ref[...])
    mix = jnp.concatenate([ret, (y * gate).astype(BF16)], axis=1)
    x1 = x + jnp.dot(mix, w_out_ref[...], preferred_element_type=F32)
    h = _rms(x1, nffn_ref[...])
    j = pl.program_id(0) % tiles_per_seq
    ext = r + 2 * HALO
    rowid = lax.broadcasted_iota(jnp.int32, (ext, 1), 0)
    first_live = jnp.where(j == 0, HALO, 0)
    end_live = jnp.where(j == tiles_per_seq - 1, r + HALO, ext)
    h = jnp.where((rowid >= first_live) & (rowid < end_live), h, 0.0).astype(BF16)
    for jj in range(D_FF // FF_TILE):
        parts = []
        for off in (jj * FF_TILE, D_FF + jj * FF_TILE):
            z = jnp.dot(h, w_up_ref[:, off:off + FF_TILE], preferred_element_type=F32)
            zc = (pltpu.roll(z, 1, 0)[HALO:HALO + r] * cw_ref[0:1, off:off + FF_TILE]
                  + z[HALO:HALO + r] * cw_ref[1:2, off:off + FF_TILE]
                  + pltpu.roll(z, ext - 1, 0)[HALO:HALO + r] * cw_ref[2:3, off:off + FF_TILE]
                  + cb_ref[:, off:off + FF_TILE])
            parts.append(zc)
        hid_scr[:, jj * FF_TILE:(jj + 1) * FF_TILE] = (jax.nn.gelu(parts[1]) * parts[0]).astype(BF16)
    x2 = x1[HALO:HALO + r] + jnp.dot(hid_scr[...], w_down_ref[...], preferred_element_type=F32)
    o_ref[...] = _rms(x2, nfin_ref[...])


def _out_ffn(x2, ret, ssm, glu_w, glu_b, w_out, norm_ffn, w_up, conv_w, conv_b, w_down,
             norm_final, seq):
    t = x2.shape[0]
    r = TOK_TILE
    tiles_per_seq = seq // r
    hb = r // HALO
    last = t // HALO - 1
    prev_blk = lambda i: jnp.maximum(i * hb - 1, 0)
    next_blk = lambda i: jnp.minimum((i + 1) * hb, last)

    def trio(width):
        return [pl.BlockSpec((r, width), lambda i: (i, 0)),
                pl.BlockSpec((HALO, width), lambda i: (prev_blk(i), 0)),
                pl.BlockSpec((HALO, width), lambda i: (next_blk(i), 0))]

    ssm_trio = [pl.BlockSpec((SSM_BLOCKS, r, LANES), lambda i: (0, i, 0)),
                pl.BlockSpec((SSM_BLOCKS, HALO, LANES), lambda i: (0, prev_blk(i), 0)),
                pl.BlockSpec((SSM_BLOCKS, HALO, LANES), lambda i: (0, next_blk(i), 0))]

    return pl.pallas_call(
        functools.partial(_out_ffn_kernel, tiles_per_seq=tiles_per_seq),
        grid=(t // r,),
        in_specs=trio(D_MODEL) + trio(RET_WIDTH) + ssm_trio + [
            _const_spec((SSM_WIDTH, SSM_WIDTH)), _const_spec((1, SSM_WIDTH)),
            _const_spec((D_MODEL, D_MODEL)), _const_spec((1, D_MODEL)),
            _const_spec((D_MODEL, 2 * D_FF)), _const_spec((3, 2 * D_FF)),
            _const_spec((1, 2 * D_FF)), _const_spec((D_FF, D_MODEL)),
            _const_spec((1, D_MODEL))],
        out_specs=pl.BlockSpec((r, D_MODEL), lambda i: (i, 0)),
        out_shape=jax.ShapeDtypeStruct((t, D_MODEL), F32),
        scratch_shapes=[pltpu.VMEM((r, D_FF), BF16)],
        compiler_params=pltpu.CompilerParams(
            dimension_semantics=("arbitrary",), vmem_limit_bytes=VMEM_LIMIT),
        name="out_ffn",
    )(x2, x2, x2, ret, ret, ret, ssm, ssm, ssm, glu_w, glu_b, w_out, norm_ffn, w_up, conv_w,
      conv_b, w_down, norm_final)


def _rotary_tables(seq):
    half = HEAD_DIM // 2
    inv_freq = ROPE_BASE ** (-jnp.arange(half, dtype=F32) / half)
    ang = jnp.arange(seq, dtype=F32)[:, None] * inv_freq[None, :]
    cos = jnp.cos(ang)
    sin = jnp.sin(ang)
    return jnp.concatenate([cos, cos], axis=1), jnp.concatenate([-sin, sin], axis=1)


def _layer(x, prep):
    b, seq, _ = x.shape
    x2 = x.reshape(b * seq, D_MODEL)
    q, k, v, g, u = _in_proj(x2, prep["norm_mix"], prep["w_in"], prep["cos"], prep["sin"], seq)
    ret = _retention(q, k, v, g, prep["ret_tabs"], prep["gn_gain"], seq)
    ssm = _s5(u, prep["s5"], seq)
    out = _out_ffn(x2, ret, ssm, prep["glu_w"], prep["glu_b"], prep["w_out"], prep["norm_ffn"],
                   prep["w_up"], prep["conv_w"], prep["conv_b"], prep["w_down"],
                   prep["norm_final"], seq)
    return out.reshape(b, seq, D_MODEL)


def kernel(x_prompt, x_sample, norm_mix, w_in, ret_gn_gain, s5_lambda_re, s5_lambda_im, s5_log_dt, s5_B_re, s5_B_im, s5_C_re, s5_C_im, s5_D, s5_glu_w, s5_glu_b, w_out, norm_ffn, w_up, conv_w, conv_b, w_down, norm_final):
    assert norm_mix.shape[0] == 1, "single-layer trunk"
    seq = x_prompt.shape[1]
    assert x_sample.shape[1] == seq
    cos, sin = _rotary_tables(seq)
    prep = {
        "norm_mix": norm_mix[0][None].astype(F32),
        "w_in": w_in[0].astype(BF16),
        "cos": cos, "sin": sin,
        "ret_tabs": _retention_tables(),
        "gn_gain": ret_gn_gain[0][None].astype(F32),
        "s5": _s5_matrices(s5_lambda_re[0], s5_lambda_im[0], s5_log_dt[0], s5_B_re[0], s5_B_im[0],
                           s5_C_re[0], s5_C_im[0], s5_D[0]),
        "glu_w": s5_glu_w[0].astype(BF16),
        "glu_b": s5_glu_b[0][None].astype(F32),
        "w_out": w_out[0].astype(BF16),
        "norm_ffn": norm_ffn[0][None].astype(F32),
        "w_up": w_up[0].astype(BF16),
        "conv_w": conv_w[0].astype(F32),
        "conv_b": conv_b[0][None].astype(F32),
        "w_down": w_down[0].astype(BF16),
        "norm_final": norm_final[None].astype(F32),
    }
    return (_layer(x_prompt, prep), _layer(x_sample, prep))
```

```python
import functools

import jax
import jax.numpy as jnp
from jax import lax
from jax.experimental import pallas as pl
from jax.experimental.pallas import tpu as pltpu

D_MODEL = 1024
RET_WIDTH = 512
SSM_WIDTH = 512
RET_HEADS = 4
HEAD_DIM = 128
RET_CHUNK = 128
ROPE_BASE = 10000.0
SSM_GROUP = 16
SSM_GROUPS = 32
SSM_STATE = 64
D_FF = 2816
EPS = 1e-6
IN_WIDTH = 4 * RET_WIDTH + SSM_WIDTH

LANES = 128
SUBLANES = 8
BF16_ROWS = 16
VMEM_LIMIT = 56 * 1024 * 1024

SSM_BLOCKS = SSM_WIDTH // LANES
GROUPS_PER_BLOCK = LANES // SSM_GROUP
STATES_PER_BLOCK = GROUPS_PER_BLOCK * SSM_STATE
SSM_CHUNK = 8
SSM_SEQS = 2
TOK_TILE = 512
MIX_TILE = 1024
FF_TILE = 256
HALO = BF16_ROWS

BF16 = jnp.bfloat16
F32 = jnp.float32


def _rms(x, w):
    return x * lax.rsqrt(jnp.mean(x * x, axis=-1, keepdims=True) + EPS) * w


def _const_spec(shape):
    nd = len(shape)
    return pl.BlockSpec(shape, lambda *_: (0,) * nd, pipeline_mode=pl.Buffered(1))


def _in_proj_kernel(x_ref, nw_ref, w_ref, cos_ref, sin_ref, q_ref, k_ref, v_ref, g_ref, u_ref):
    h = _rms(x_ref[...], nw_ref[...]).astype(BF16)
    proj = jnp.dot(h, w_ref[...], preferred_element_type=F32)
    cos = cos_ref[...]
    sin = sin_ref[...]
    kscale = HEAD_DIM ** -0.5
    for hd in range(RET_HEADS):
        lo = hd * HEAD_DIM
        qh = proj[:, lo:lo + HEAD_DIM]
        q_ref[:, lo:lo + HEAD_DIM] = (qh * cos + pltpu.roll(qh, HEAD_DIM // 2, 1) * sin).astype(BF16)
        kh = proj[:, RET_WIDTH + lo:RET_WIDTH + lo + HEAD_DIM]
        kr = kh * cos + pltpu.roll(kh, HEAD_DIM // 2, 1) * sin
        k_ref[:, lo:lo + HEAD_DIM] = (kr * kscale).astype(BF16)
    v_ref[...] = proj[:, 2 * RET_WIDTH:3 * RET_WIDTH].astype(BF16)
    g_ref[...] = proj[:, 3 * RET_WIDTH:4 * RET_WIDTH].astype(BF16)
    for blk in range(SSM_BLOCKS):
        lo = 4 * RET_WIDTH + blk * LANES
        u_ref[blk] = proj[:, lo:lo + LANES]


def _in_proj(x2, norm_w, w_in, cos2, sin2, seq):
    t = x2.shape[0]
    r = TOK_TILE
    tiles_per_seq = seq // r
    tok = lambda i: (i, 0)
    out_tok = jax.ShapeDtypeStruct((t, RET_WIDTH), BF16)
    return pl.pallas_call(
        _in_proj_kernel,
        grid=(t // r,),
        in_specs=[
            pl.BlockSpec((r, D_MODEL), tok),
            _const_spec((1, D_MODEL)),
            _const_spec((D_MODEL, IN_WIDTH)),
            pl.BlockSpec((r, HEAD_DIM), lambda i: (i % tiles_per_seq, 0)),
            pl.BlockSpec((r, HEAD_DIM), lambda i: (i % tiles_per_seq, 0)),
        ],
        out_specs=[
            pl.BlockSpec((r, RET_WIDTH), tok),
            pl.BlockSpec((r, RET_WIDTH), tok),
            pl.BlockSpec((r, RET_WIDTH), tok),
            pl.BlockSpec((r, RET_WIDTH), tok),
            pl.BlockSpec((SSM_BLOCKS, r, LANES), lambda i: (0, i, 0)),
        ],
        out_shape=[out_tok, out_tok, out_tok, out_tok,
                   jax.ShapeDtypeStruct((SSM_BLOCKS, t, LANES), F32)],
        compiler_params=pltpu.CompilerParams(
            dimension_semantics=("arbitrary",), vmem_limit_bytes=VMEM_LIMIT),
        name="in_proj",
    )(x2, norm_w, w_in, cos2, sin2)


def _retention_kernel(q_ref, k_ref, v_ref, g_ref, tab_ref, row_ref, gain_ref, o_ref,
                      kv_scr, r_scr):
    c = RET_CHUNK
    nchunks = q_ref.shape[0] // c
    heads = [slice(hd * HEAD_DIM, (hd + 1) * HEAD_DIM) for hd in range(RET_HEADS)]

    def summarize(n, carry):
        rows = pl.ds(pl.multiple_of(n * c, c), c)
        for hd, cols in enumerate(heads):
            kn = k_ref[rows, cols]
            kcat = jnp.concatenate([kn * row_ref[hd, 0], kn * row_ref[hd, 1]], axis=1)
            kv_scr[n, hd] = lax.dot_general(kcat, v_ref[rows, cols], (((0,), (0,)), ((), ())),
                                            preferred_element_type=F32)
        return carry

    lax.fori_loop(0, nchunks, summarize, 0, unroll=2)

    for hd in range(RET_HEADS):
        cd = tab_ref[hd, 1]
        rf = jnp.zeros((HEAD_DIM, HEAD_DIM), F32)
        for n in range(nchunks):
            r_scr[n, hd, 0:HEAD_DIM, :] = rf.astype(BF16)
            rf = cd * rf + kv_scr[n, hd, 0:HEAD_DIM, :]
        rb = jnp.zeros((HEAD_DIM, HEAD_DIM), F32)
        for n in reversed(range(nchunks)):
            r_scr[n, hd, HEAD_DIM:2 * HEAD_DIM, :] = rb.astype(BF16)
            rb = cd * rb + kv_scr[n, hd, HEAD_DIM:2 * HEAD_DIM, :]

    def emit(n, carry):
        rows = pl.ds(pl.multiple_of(n * c, c), c)
        for hd, cols in enumerate(heads):
            qn = q_ref[rows, cols]
            s = lax.dot_general(qn, k_ref[rows, cols], (((1,), (1,)), ((), ())),
                                preferred_element_type=F32) * tab_ref[hd, 0]
            inner = jnp.dot(s.astype(BF16), v_ref[rows, cols], preferred_element_type=F32)
            qcat = jnp.concatenate([qn * row_ref[hd, 2], qn * row_ref[hd, 3]], axis=1)
            o = inner + jnp.dot(qcat, r_scr[n, hd], preferred_element_type=F32)
            o = o * lax.rsqrt(jnp.mean(o * o, axis=-1, keepdims=True) + EPS) * gain_ref[:, cols]
            o_ref[rows, cols] = (jax.nn.silu(g_ref[rows, cols].astype(F32)) * o).astype(BF16)
        return carry

    lax.fori_loop(0, nchunks, emit, 0, unroll=4)


def _retention(q, k, v, g, tabs, gain, seq):
    tab_f32, tab_rows = tabs
    t = q.shape[0]
    nchunks = seq // RET_CHUNK
    blk = pl.BlockSpec((seq, RET_WIDTH), lambda b: (b, 0))
    return pl.pallas_call(
        _retention_kernel,
        grid=(t // seq,),
        in_specs=[blk, blk, blk, blk,
                  _const_spec((RET_HEADS, 2, RET_CHUNK, HEAD_DIM)),
                  _const_spec((RET_HEADS, 4, RET_CHUNK, HEAD_DIM)),
                  _const_spec((1, RET_WIDTH))],
        out_specs=blk,
        out_shape=jax.ShapeDtypeStruct((t, RET_WIDTH), BF16),
        scratch_shapes=[pltpu.VMEM((nchunks, RET_HEADS, 2 * HEAD_DIM, HEAD_DIM), F32),
                        pltpu.VMEM((nchunks, RET_HEADS, 2 * HEAD_DIM, HEAD_DIM), BF16)],
        compiler_params=pltpu.CompilerParams(
            dimension_semantics=("arbitrary",), vmem_limit_bytes=VMEM_LIMIT),
        name="retention",
    )(q, k, v, g, tab_f32, tab_rows, gain)


def _retention_tables():
    c = RET_CHUNK
    lg = jnp.log(1.0 - 2.0 ** (-5.0 - jnp.arange(RET_HEADS, dtype=F32)))[:, None, None]
    idx = jnp.arange(c, dtype=F32)
    col = jnp.broadcast_to(idx[:, None], (c, HEAD_DIM))[None]
    dmat = jnp.exp(lg * jnp.abs(idx[:, None] - idx[None, :])[None])
    cd = jnp.exp(lg * c) * jnp.ones((1, c, HEAD_DIM), F32)
    zf = jnp.exp(lg * (c - 1.0 - col))
    zb = jnp.exp(lg * col)
    xf = jnp.exp(lg * (col + 1.0))
    xb = jnp.exp(lg * (c - col))
    return jnp.stack([dmat, cd], axis=1), jnp.stack([zf, zb, xf, xb], axis=1).astype(BF16)


def _cmul(ar, ai, br, bi):
    return ar * br - ai * bi, ar * bi + ai * br


def _lane_group(shape, width):
    lane = lax.broadcasted_iota(jnp.int32, shape, 1)
    return (lane >> (width.bit_length() - 1)) & (GROUPS_PER_BLOCK - 1)


def _s5_kernel(u_ref, mc_ref, wc_ref, vc_ref, tab_ref, y_ref,
               m_scr, w_scr, v_scr, s_scr, x_scr, *, nseq, nrows):
    c = SSM_CHUNK
    sp = STATES_PER_BLOCK
    rows_all = nseq * nrows

    @pl.when(pl.program_id(1) == 0)
    def _expand():
        grp_m = _lane_group((SSM_GROUP, c * LANES), SSM_GROUP)
        grp_w = _lane_group((SSM_GROUP, 4 * sp), SSM_STATE)
        grp_v = _lane_group((SSM_STATE, c * LANES), SSM_GROUP)
        for g in range(GROUPS_PER_BLOCK):
            for s in range(c):
                src = slice(s * SSM_GROUP, (s + 1) * SSM_GROUP)
                dst = slice(s * LANES + g * SSM_GROUP, s * LANES + (g + 1) * SSM_GROUP)
                m_scr[dst, :] = jnp.where(grp_m == g, mc_ref[src, :], 0)
                w_scr[dst, :] = jnp.where(grp_w == g, wc_ref[src, :], 0)
            for dp in range(4):
                dst = slice(dp * sp + g * SSM_STATE, dp * sp + (g + 1) * SSM_STATE)
                v_scr[dst, :] = jnp.where(
                    grp_v == g, vc_ref[dp * SSM_STATE:(dp + 1) * SSM_STATE, :], 0)

    u = jnp.concatenate(
        [u_ref[pl.ds(s, rows_all, stride=c), :].astype(BF16) for s in range(c)], axis=1)
    s_scr[...] = jnp.dot(u, w_scr[...], preferred_element_type=F32)
    nblk = nrows // SUBLANES
    row = lax.broadcasted_iota(jnp.int32, (SUBLANES, sp), 0)

    def scan_block(base, d, blk, carry):
        rows = pl.ds(pl.multiple_of(base + blk * SUBLANES, SUBLANES), SUBLANES)
        lo = d * 2 * sp
        pr = s_scr[rows, lo:lo + sp]
        pi = s_scr[rows, lo + sp:lo + 2 * sp]
        for step, sh in enumerate((1, 2, 4)):
            shift = sh if d == 0 else SUBLANES - sh
            tr, ti = _cmul(tab_ref[d, step, 0], tab_ref[d, step, 1],
                           pltpu.roll(pr, shift, 0), pltpu.roll(pi, shift, 0))
            pr = pr + tr
            pi = pi + ti
        cr, ci = carry
        tr, ti = _cmul(tab_ref[d, 3, 0], tab_ref[d, 3, 1], cr, ci)
        xr = pr + tr
        xi = pi + ti
        edge = 0 if d == 0 else SUBLANES - 1
        shift = 1 if d == 0 else SUBLANES - 1
        x_scr[rows, lo:lo + sp] = jnp.where(row == edge, cr, pltpu.roll(xr, shift, 0))
        x_scr[rows, lo + sp:lo + 2 * sp] = jnp.where(row == edge, ci, pltpu.roll(xi, shift, 0))
        last = SUBLANES - 1 if d == 0 else 0
        return (jnp.broadcast_to(xr[last:last + 1, :], (SUBLANES, sp)),
                jnp.broadcast_to(xi[last:last + 1, :], (SUBLANES, sp)))

    zero = jnp.zeros((SUBLANES, sp), F32)

    def body(i, carry):
        out = ()
        for sq in range(nseq):
            cf = scan_block(sq * nrows, 0, i, carry[4 * sq:4 * sq + 2])
            cb = scan_block(sq * nrows, 1, nblk - 1 - i, carry[4 * sq + 2:4 * sq + 4])
            out = out + cf + cb
        return out

    lax.fori_loop(0, nblk, body, (zero,) * (4 * nseq))

    y = jnp.dot(u, m_scr[...], preferred_element_type=F32)
    y = y + jnp.dot(x_scr[...].astype(BF16), v_scr[...], preferred_element_type=F32)
    for s in range(c):
        y_ref[pl.ds(s, rows_all, stride=c), :] = y[:, s * LANES:(s + 1) * LANES]


def _s5(u_blocks, mats, seq):
    mc, wc, vc, tabs = mats
    nb, t, _ = u_blocks.shape
    c = SSM_CHUNK
    nrows = seq // c
    rows = SSM_SEQS * nrows
    width = c * LANES
    sp = STATES_PER_BLOCK
    return pl.pallas_call(
        functools.partial(_s5_kernel, nseq=SSM_SEQS, nrows=nrows),
        grid=(nb, t // (c * rows)),
        in_specs=[
            pl.BlockSpec((None, rows * c, LANES), lambda b, i: (b, i, 0)),
            pl.BlockSpec((None, c * SSM_GROUP, width), lambda b, i: (b, 0, 0)),
            pl.BlockSpec((None, c * SSM_GROUP, 4 * sp), lambda b, i: (b, 0, 0)),
            pl.BlockSpec((None, 4 * SSM_STATE, width), lambda b, i: (b, 0, 0)),
            pl.BlockSpec((None, 2, 4, 2, SUBLANES, sp), lambda b, i: (b, 0, 0, 0, 0, 0)),
        ],
        out_specs=pl.BlockSpec((None, rows * c, LANES), lambda b, i: (b, i, 0)),
        out_shape=jax.ShapeDtypeStruct((nb, t, LANES), F32),
        scratch_shapes=[pltpu.VMEM((width, width), BF16),
                        pltpu.VMEM((width, 4 * sp), BF16),
                        pltpu.VMEM((4 * sp, width), BF16),
                        pltpu.VMEM((rows, 4 * sp), F32),
                        pltpu.VMEM((rows, 4 * sp), F32)],
        compiler_params=pltpu.CompilerParams(
            dimension_semantics=("arbitrary", "arbitrary"), vmem_limit_bytes=VMEM_LIMIT),
        name="s5",
    )(u_blocks, mc, wc, vc, tabs)


def _s5_matrices(lam_re, lam_im, log_dt, b_re, b_im, c_re, c_im, d_skip):
    c = SSM_CHUNK
    nb, gpb, ch, ns = SSM_BLOCKS, GROUPS_PER_BLOCK, SSM_GROUP, SSM_STATE
    hp = lax.Precision.HIGHEST
    lr = jnp.minimum(lam_re.astype(F32), -1e-4)
    li = lam_im.astype(F32)
    dt = jnp.exp(log_dt.astype(F32))[..., None]
    p = jnp.arange(SUBLANES * c + 1, dtype=F32)[:, None, None, None]
    pmag = jnp.exp(p * (lr * dt)[None])
    pr = pmag * jnp.cos(p * (li * dt)[None])
    pi = pmag * jnp.sin(p * (li * dt)[None])
    ar, ai = pr[1], pi[1]
    den = lr * lr + li * li
    nr = ar - 1.0
    ni = ai
    cr = (nr * lr + ni * li) / den
    ci = (ni * lr - nr * li) / den
    bbr = cr[..., None] * b_re - ci[..., None] * b_im
    bbi = cr[..., None] * b_im + ci[..., None] * b_re

    er = c_re[None] * pr[:c, :, :, None, :] - c_im[None] * pi[:c, :, :, None, :]
    ei = c_re[None] * pi[:c, :, :, None, :] + c_im[None] * pr[:c, :, :, None, :]
    kern = (jnp.einsum("pdgon,dgni->pdgio", er, bbr, precision=hp)
            - jnp.einsum("pdgon,dgni->pdgio", ei, bbi, precision=hp))
    eye_ch = jnp.eye(ch, dtype=F32)
    diag = kern[0, 0] + kern[0, 1] + d_skip.reshape(SSM_GROUPS, ch)[:, :, None] * eye_ch
    tok = jnp.arange(c)
    lag = tok[None, :] - tok[:, None]
    ksel = jnp.where((lag > 0)[..., None, None, None], kern[jnp.abs(lag), 0],
                     jnp.where((lag < 0)[..., None, None, None], kern[jnp.abs(lag), 1],
                               diag[None, None]))
    mc = ksel.reshape(c, c, nb, gpb, ch, ch).transpose(2, 0, 4, 1, 3, 5).reshape(
        nb, c * ch, c * LANES)

    pw = jnp.stack([pr[c - 1 - tok, 0], pr[tok, 1]], axis=1)
    pwi = jnp.stack([pi[c - 1 - tok, 0], pi[tok, 1]], axis=1)
    wr = pw[..., None] * bbr[None] - pwi[..., None] * bbi[None]
    wi = pw[..., None] * bbi[None] + pwi[..., None] * bbr[None]
    wcat = jnp.stack([wr, wi], axis=2)
    wc = wcat.reshape(c, 2, 2, nb, gpb, ns, ch).transpose(3, 0, 6, 1, 2, 4, 5).reshape(
        nb, c * ch, 4 * STATES_PER_BLOCK)

    qr = jnp.stack([pr[tok + 1, 0], pr[c - tok, 1]], axis=1)
    qi = jnp.stack([pi[tok + 1, 0], pi[c - tok, 1]], axis=1)
    vr = c_re[None] * qr[:, :, :, None, :] - c_im[None] * qi[:, :, :, None, :]
    vi = -(c_re[None] * qi[:, :, :, None, :] + c_im[None] * qr[:, :, :, None, :])
    vcat = jnp.stack([vr, vi], axis=2)
    vc = vcat.reshape(c, 2, 2, nb, gpb, ch, ns).transpose(3, 1, 2, 6, 0, 4, 5).reshape(
        nb, 4 * ns, c * LANES)

    rowi = jnp.arange(SUBLANES)

    def lanes(x):
        return x.reshape(x.shape[:-2] + (nb, STATES_PER_BLOCK))

    tabs = []
    for d in range(2):
        per_dir = []
        for sh in (1, 2, 4):
            keep = (rowi >= sh) if d == 0 else (rowi < SUBLANES - sh)
            per_dir.append(jnp.stack([
                jnp.where(keep[:, None, None], lanes(pr[sh * c, d])[None], 0.0),
                jnp.where(keep[:, None, None], lanes(pi[sh * c, d])[None], 0.0)]))
        cpow = (rowi + 1) * c if d == 0 else (SUBLANES - rowi) * c
        per_dir.append(jnp.stack([lanes(pr[cpow, d]), lanes(pi[cpow, d])]))
        tabs.append(jnp.stack(per_dir))
    tabs = jnp.stack(tabs).transpose(4, 0, 1, 2, 3, 5)
    return mc.astype(BF16), wc.astype(BF16), vc.astype(BF16), tabs


def _mix_norm_kernel(x_ref, ret_ref, ssm_ref, glu_w_ref, glu_b_ref, w_out_ref, nffn_ref,
                     x1_ref, h_ref):
    ssm = jnp.concatenate([ssm_ref[b] for b in range(SSM_BLOCKS)], axis=1)
    y = jax.nn.gelu(ssm)
    gate = jax.nn.sigmoid(
        jnp.dot(y.astype(BF16), glu_w_ref[...], preferred_element_type=F32) + glu_b_ref[...])
    mix = jnp.concatenate([ret_ref[...], (y * gate).astype(BF16)], axis=1)
    x1 = x_ref[...] + jnp.dot(mix, w_out_ref[...], preferred_element_type=F32)
    x1_ref[...] = x1
    h_ref[...] = _rms(x1, nffn_ref[...]).astype(BF16)


def _mix_norm(x2, ret, ssm, glu_w, glu_b, w_out, norm_ffn):
    t = x2.shape[0]
    r = MIX_TILE
    tok = lambda i: (i, 0)
    return pl.pallas_call(
        _mix_norm_kernel,
        grid=(t // r,),
        in_specs=[pl.BlockSpec((r, D_MODEL), tok), pl.BlockSpec((r, RET_WIDTH), tok),
                  pl.BlockSpec((SSM_BLOCKS, r, LANES), lambda i: (0, i, 0)),
                  _const_spec((SSM_WIDTH, SSM_WIDTH)), _const_spec((1, SSM_WIDTH)),
                  _const_spec((D_MODEL, D_MODEL)), _const_spec((1, D_MODEL))],
        out_specs=[pl.BlockSpec((r, D_MODEL), tok), pl.BlockSpec((r, D_MODEL), tok)],
        out_shape=[jax.ShapeDtypeStruct((t, D_MODEL), F32),
                   jax.ShapeDtypeStruct((t, D_MODEL), BF16)],
        compiler_params=pltpu.CompilerParams(
            dimension_semantics=("arbitrary",), vmem_limit_bytes=VMEM_LIMIT),
        name="mix_norm",
    )(x2, ret, ssm, glu_w, glu_b, w_out, norm_ffn)


def _conv_ffn_kernel(x1_ref, hm_ref, hp_ref, hn_ref, w_up_ref, cw_ref, cb_ref, w_down_ref,
                     nfin_ref, o_ref, hid_scr, *, tiles_per_seq):
    r = hm_ref.shape[0]
    ext = r + 2 * HALO
    j = pl.program_id(0) % tiles_per_seq
    hp = jnp.where(j == 0, jnp.zeros(hp_ref.shape, BF16), hp_ref[...])
    hn = jnp.where(j == tiles_per_seq - 1, jnp.zeros(hn_ref.shape, BF16), hn_ref[...])
    h = jnp.concatenate([hp, hm_ref[...], hn], axis=0)
    for jj in range(D_FF // FF_TILE):
        parts = []
        for off in (jj * FF_TILE, D_FF + jj * FF_TILE):
            z = jnp.dot(h, w_up_ref[:, off:off + FF_TILE], preferred_element_type=F32)
            zc = (pltpu.roll(z, 1, 0)[HALO:HALO + r] * cw_ref[0:1, off:off + FF_TILE]
                  + z[HALO:HALO + r] * cw_ref[1:2, off:off + FF_TILE]
                  + pltpu.roll(z, ext - 1, 0)[HALO:HALO + r] * cw_ref[2:3, off:off + FF_TILE]
                  + cb_ref[:, off:off + FF_TILE])
            parts.append(zc)
        hid_scr[:, jj * FF_TILE:(jj + 1) * FF_TILE] = (jax.nn.gelu(parts[1]) * parts[0]).astype(BF16)
    x2 = x1_ref[...] + jnp.dot(hid_scr[...], w_down_ref[...], preferred_element_type=F32)
    o_ref[...] = _rms(x2, nfin_ref[...])


def _conv_ffn(x1, h, w_up, conv_w, conv_b, w_down, norm_final, seq):
    t = x1.shape[0]
    r = TOK_TILE
    tiles_per_seq = seq // r
    hb = r // HALO
    last = t // HALO - 1
    tok = lambda i: (i, 0)
    return pl.pallas_call(
        functools.partial(_conv_ffn_kernel, tiles_per_seq=tiles_per_seq),
        grid=(t // r,),
        in_specs=[pl.BlockSpec((r, D_MODEL), tok), pl.BlockSpec((r, D_MODEL), tok),
                  pl.BlockSpec((HALO, D_MODEL), lambda i: (jnp.maximum(i * hb - 1, 0), 0)),
                  pl.BlockSpec((HALO, D_MODEL), lambda i: (jnp.minimum((i + 1) * hb, last), 0)),
                  _const_spec((D_MODEL, 2 * D_FF)), _const_spec((3, 2 * D_FF)),
                  _const_spec((1, 2 * D_FF)), _const_spec((D_FF, D_MODEL)),
                  _const_spec((1, D_MODEL))],
        out_specs=pl.BlockSpec((r, D_MODEL), tok),
        out_shape=jax.ShapeDtypeStruct((t, D_MODEL), F32),
        scratch_shapes=[pltpu.VMEM((r, D_FF), BF16)],
        compiler_params=pltpu.CompilerParams(
            dimension_semantics=("arbitrary",), vmem_limit_bytes=VMEM_LIMIT),
        name="conv_ffn",
    )(x1, h, h, h, w_up, conv_w, conv_b, w_down, norm_final)


def _rotary_tables(seq):
    half = HEAD_DIM // 2
    inv_freq = ROPE_BASE ** (-jnp.arange(half, dtype=F32) / half)
    ang = jnp.arange(seq, dtype=F32)[:, None] * inv_freq[None, :]
    cos = jnp.cos(ang)
    sin = jnp.sin(ang)
    return jnp.concatenate([cos, cos], axis=1), jnp.concatenate([-sin, sin], axis=1)


def _layer(x, prep):
    b, seq, _ = x.shape
    x2 = x.reshape(b * seq, D_MODEL)
    q, k, v, g, u = _in_proj(x2, prep["norm_mix"], prep["w_in"], prep["cos"], prep["sin"], seq)
    ret = _retention(q, k, v, g, prep["ret_tabs"], prep["gn_gain"], seq)
    ssm = _s5(u, prep["s5"], seq)
    x1, h = _mix_norm(x2, ret, ssm, prep["glu_w"], prep["glu_b"], prep["w_out"], prep["norm_ffn"])
    out = _conv_ffn(x1, h, prep["w_up"], prep["conv_w"], prep["conv_b"], prep["w_down"],
                    prep["norm_final"], seq)
    return out.reshape(b, seq, D_MODEL)


def kernel(x_prompt, x_sample, norm_mix, w_in, ret_gn_gain, s5_lambda_re, s5_lambda_im, s5_log_dt, s5_B_re, s5_B_im, s5_C_re, s5_C_im, s5_D, s5_glu_w, s5_glu_b, w_out, norm_ffn, w_up, conv_w, conv_b, w_down, norm_final):
    assert norm_mix.shape[0] == 1, "single-layer trunk"
    seq = x_prompt.shape[1]
    assert x_sample.shape[1] == seq
    cos, sin = _rotary_tables(seq)
    prep = {
        "norm_mix": norm_mix[0][None].astype(F32),
        "w_in": w_in[0].astype(BF16),
        "cos": cos, "sin": sin,
        "ret_tabs": _retention_tables(),
        "gn_gain": ret_gn_gain[0][None].astype(F32),
        "s5": _s5_matrices(s5_lambda_re[0], s5_lambda_im[0], s5_log_dt[0], s5_B_re[0], s5_B_im[0],
                           s5_C_re[0], s5_C_im[0], s5_D[0]),
        "glu_w": s5_glu_w[0].astype(BF16),
        "glu_b": s5_glu_b[0][None].astype(F32),
        "w_out": w_out[0].astype(BF16),
        "norm_ffn": norm_ffn[0][None].astype(F32),
        "w_up": w_up[0].astype(BF16),
        "conv_w": conv_w[0].astype(F32),
        "conv_b": conv_b[0][None].astype(F32),
        "w_down": w_down[0].astype(BF16),
        "norm_final": norm_final[None].astype(F32),
    }
    return (_layer(x_prompt, prep), _layer(x_sample, prep))
```

```python
import functools

import jax
import jax.numpy as jnp
from jax import lax
from jax.experimental import pallas as pl
from jax.experimental.pallas import tpu as pltpu

D_MODEL = 1024
RET_WIDTH = 512
SSM_WIDTH = 512
RET_HEADS = 4
HEAD_DIM = 128
RET_CHUNK = 128
ROPE_BASE = 10000.0
SSM_GROUP = 16
SSM_GROUPS = 32
SSM_STATE = 64
D_FF = 2816
EPS = 1e-6
IN_WIDTH = 4 * RET_WIDTH + SSM_WIDTH

LANES = 128
SUBLANES = 8
BF16_ROWS = 16
VMEM_LIMIT = 56 * 1024 * 1024

SSM_BLOCKS = SSM_WIDTH // LANES
GROUPS_PER_BLOCK = LANES // SSM_GROUP
SSM_CHUNK = 16
SSM_SEQS = 4
TOK_TILE = 512
MIX_TILE = 1024
FF_TILE = 256
HALO = BF16_ROWS

BF16 = jnp.bfloat16
F32 = jnp.float32


def _rms(x, w):
    return x * lax.rsqrt(jnp.mean(x * x, axis=-1, keepdims=True) + EPS) * w


def _const_spec(shape):
    nd = len(shape)
    return pl.BlockSpec(shape, lambda *_: (0,) * nd, pipeline_mode=pl.Buffered(1))


def _in_proj_kernel(x_ref, nw_ref, w_ref, cos_ref, sin_ref, q_ref, k_ref, v_ref, g_ref, u_ref):
    h = _rms(x_ref[...], nw_ref[...]).astype(BF16)
    proj = jnp.dot(h, w_ref[...], preferred_element_type=F32)
    cos = cos_ref[...]
    sin = sin_ref[...]
    kscale = HEAD_DIM ** -0.5
    for hd in range(RET_HEADS):
        lo = hd * HEAD_DIM
        qh = proj[:, lo:lo + HEAD_DIM]
        q_ref[:, lo:lo + HEAD_DIM] = (qh * cos + pltpu.roll(qh, HEAD_DIM // 2, 1) * sin).astype(BF16)
        kh = proj[:, RET_WIDTH + lo:RET_WIDTH + lo + HEAD_DIM]
        kr = kh * cos + pltpu.roll(kh, HEAD_DIM // 2, 1) * sin
        k_ref[:, lo:lo + HEAD_DIM] = (kr * kscale).astype(BF16)
    v_ref[...] = proj[:, 2 * RET_WIDTH:3 * RET_WIDTH].astype(BF16)
    g_ref[...] = proj[:, 3 * RET_WIDTH:4 * RET_WIDTH].astype(BF16)
    for blk in range(SSM_BLOCKS):
        lo = 4 * RET_WIDTH + blk * LANES
        u_ref[blk] = proj[:, lo:lo + LANES]


def _in_proj(x2, norm_w, w_in, cos2, sin2, seq):
    t = x2.shape[0]
    r = TOK_TILE
    tiles_per_seq = seq // r
    tok = lambda i: (i, 0)
    out_tok = jax.ShapeDtypeStruct((t, RET_WIDTH), BF16)
    return pl.pallas_call(
        _in_proj_kernel,
        grid=(t // r,),
        in_specs=[
            pl.BlockSpec((r, D_MODEL), tok),
            _const_spec((1, D_MODEL)),
            _const_spec((D_MODEL, IN_WIDTH)),
            pl.BlockSpec((r, HEAD_DIM), lambda i: (i % tiles_per_seq, 0)),
            pl.BlockSpec((r, HEAD_DIM), lambda i: (i % tiles_per_seq, 0)),
        ],
        out_specs=[
            pl.BlockSpec((r, RET_WIDTH), tok),
            pl.BlockSpec((r, RET_WIDTH), tok),
            pl.BlockSpec((r, RET_WIDTH), tok),
            pl.BlockSpec((r, RET_WIDTH), tok),
            pl.BlockSpec((SSM_BLOCKS, r, LANES), lambda i: (0, i, 0)),
        ],
        out_shape=[out_tok, out_tok, out_tok, out_tok,
                   jax.ShapeDtypeStruct((SSM_BLOCKS, t, LANES), F32)],
        compiler_params=pltpu.CompilerParams(
            dimension_semantics=("arbitrary",), vmem_limit_bytes=VMEM_LIMIT),
        name="in_proj",
    )(x2, norm_w, w_in, cos2, sin2)


def _retention_kernel(q_ref, k_ref, v_ref, g_ref, tab_ref, row_ref, gain_ref, o_ref,
                      kv_scr, r_scr):
    c = RET_CHUNK
    nchunks = q_ref.shape[0] // c
    heads = [slice(hd * HEAD_DIM, (hd + 1) * HEAD_DIM) for hd in range(RET_HEADS)]

    def summarize(n, carry):
        rows = pl.ds(pl.multiple_of(n * c, c), c)
        for hd, cols in enumerate(heads):
            kn = k_ref[rows, cols]
            kcat = jnp.concatenate([kn * row_ref[hd, 0], kn * row_ref[hd, 1]], axis=1)
            kv_scr[n, hd] = lax.dot_general(kcat, v_ref[rows, cols], (((0,), (0,)), ((), ())),
                                            preferred_element_type=F32)
        return carry

    lax.fori_loop(0, nchunks, summarize, 0, unroll=2)

    for hd in range(RET_HEADS):
        cd = tab_ref[hd, 1]
        rf = jnp.zeros((HEAD_DIM, HEAD_DIM), F32)
        for n in range(nchunks):
            r_scr[n, hd, 0:HEAD_DIM, :] = rf.astype(BF16)
            rf = cd * rf + kv_scr[n, hd, 0:HEAD_DIM, :]
        rb = jnp.zeros((HEAD_DIM, HEAD_DIM), F32)
        for n in reversed(range(nchunks)):
            r_scr[n, hd, HEAD_DIM:2 * HEAD_DIM, :] = rb.astype(BF16)
            rb = cd * rb + kv_scr[n, hd, HEAD_DIM:2 * HEAD_DIM, :]

    def emit(n, carry):
        rows = pl.ds(pl.multiple_of(n * c, c), c)
        for hd, cols in enumerate(heads):
            qn = q_ref[rows, cols]
            s = lax.dot_general(qn, k_ref[rows, cols], (((1,), (1,)), ((), ())),
                                preferred_element_type=F32) * tab_ref[hd, 0]
            inner = jnp.dot(s.astype(BF16), v_ref[rows, cols], preferred_element_type=F32)
            qcat = jnp.concatenate([qn * row_ref[hd, 2], qn * row_ref[hd, 3]], axis=1)
            o = inner + jnp.dot(qcat, r_scr[n, hd], preferred_element_type=F32)
            o = o * lax.rsqrt(jnp.mean(o * o, axis=-1, keepdims=True) + EPS) * gain_ref[:, cols]
            o_ref[rows, cols] = (jax.nn.silu(g_ref[rows, cols].astype(F32)) * o).astype(BF16)
        return carry

    lax.fori_loop(0, nchunks, emit, 0, unroll=4)


def _retention(q, k, v, g, tabs, gain, seq):
    tab_f32, tab_rows = tabs
    t = q.shape[0]
    nchunks = seq // RET_CHUNK
    blk = pl.BlockSpec((seq, RET_WIDTH), lambda b: (b, 0))
    return pl.pallas_call(
        _retention_kernel,
        grid=(t // seq,),
        in_specs=[blk, blk, blk, blk,
                  _const_spec((RET_HEADS, 2, RET_CHUNK, HEAD_DIM)),
                  _const_spec((RET_HEADS, 4, RET_CHUNK, HEAD_DIM)),
                  _const_spec((1, RET_WIDTH))],
        out_specs=blk,
        out_shape=jax.ShapeDtypeStruct((t, RET_WIDTH), BF16),
        scratch_shapes=[pltpu.VMEM((nchunks, RET_HEADS, 2 * HEAD_DIM, HEAD_DIM), F32),
                        pltpu.VMEM((nchunks, RET_HEADS, 2 * HEAD_DIM, HEAD_DIM), BF16)],
        compiler_params=pltpu.CompilerParams(
            dimension_semantics=("arbitrary",), vmem_limit_bytes=VMEM_LIMIT),
        name="retention",
    )(q, k, v, g, tab_f32, tab_rows, gain)


def _retention_tables():
    c = RET_CHUNK
    lg = jnp.log(1.0 - 2.0 ** (-5.0 - jnp.arange(RET_HEADS, dtype=F32)))[:, None, None]
    idx = jnp.arange(c, dtype=F32)
    col = jnp.broadcast_to(idx[:, None], (c, HEAD_DIM))[None]
    dmat = jnp.exp(lg * jnp.abs(idx[:, None] - idx[None, :])[None])
    cd = jnp.exp(lg * c) * jnp.ones((1, c, HEAD_DIM), F32)
    zf = jnp.exp(lg * (c - 1.0 - col))
    zb = jnp.exp(lg * col)
    xf = jnp.exp(lg * (col + 1.0))
    xb = jnp.exp(lg * (c - col))
    return jnp.stack([dmat, cd], axis=1), jnp.stack([zf, zb, xf, xb], axis=1).astype(BF16)


def _cmul(ar, ai, br, bi):
    return ar * br - ai * bi, ar * bi + ai * br


def _roll_lanes(x, shift):
    ntiles = x.shape[1] // LANES
    whole, part = divmod(shift % x.shape[1], LANES)
    tiles = [x[:, k * LANES:(k + 1) * LANES] for k in range(ntiles)]
    tiles = [tiles[(k - whole) % ntiles] for k in range(ntiles)]
    if part:
        lane = lax.broadcasted_iota(jnp.int32, tiles[0].shape, 1)
        rolled = [pltpu.roll(t, part, 1) for t in tiles]
        tiles = [jnp.where(lane < part, rolled[(k - 1) % ntiles], rolled[k]) for k in range(ntiles)]
    return tiles[0] if ntiles == 1 else jnp.concatenate(tiles, axis=1)


def _split_dot(a, z):
    dims = (((1,), (1,)), ((), ()))
    a_hi = a.astype(BF16)
    z_hi = z.astype(BF16)
    a_lo = (a - a_hi.astype(F32)).astype(BF16)
    z_lo = (z - z_hi.astype(F32)).astype(BF16)
    return (lax.dot_general(a_hi, z_hi, dims, preferred_element_type=F32)
            + lax.dot_general(a_hi, z_lo, dims, preferred_element_type=F32)
            + lax.dot_general(a_lo, z_hi, dims, preferred_element_type=F32))


def _s5_kernel(u_ref, perm_ref, bt_ref, ct_ref, pw_ref, dt0_ref, tab_ref, y_ref,
               m_scr, w_scr, vt_scr, g_scr, u_scr, s_scr, x_scr, *, nseq, seq):
    c = SSM_CHUNK
    sub = c // SUBLANES
    gpb = GROUPS_PER_BLOCK
    npair = gpb // 2
    width = c * SSM_GROUP
    sp = npair * LANES
    rows8 = nseq * seq // SUBLANES
    rows = rows8 // sub
    nrows = seq // c

    @pl.when(pl.program_id(1) == 0)
    def _expand():
        slot = lax.broadcasted_iota(jnp.int32, (SSM_GROUP, width), 1) >> 4
        for g in range(gpb):
            pair, gi = divmod(g, 2)
            zt = ([], [])
            for d in range(2):
                b_re, b_im = bt_ref[d, g, 0], bt_ref[d, g, 1]
                c_re, c_im = ct_ref[d, g, 0], ct_ref[d, g, 1]
                for s in range(c):
                    r16 = slice(gi * width + s * SSM_GROUP, gi * width + (s + 1) * SSM_GROUP)
                    re_cols = slice((2 * d) * LANES, (2 * d + 1) * LANES)
                    im_cols = slice((2 * d + 1) * LANES, (2 * d + 2) * LANES)
                    p_w = c - 1 - s if d == 0 else s
                    p_v = s + 1 if d == 0 else c - s
                    p_z = s if d == 0 else c - 1 - s
                    wr, wi = _cmul(b_re, b_im, pw_ref[p_w, d, pair, 0], pw_ref[p_w, d, pair, 1])
                    w_scr[pair, r16, re_cols] = wr.astype(BF16)
                    w_scr[pair, r16, im_cols] = wi.astype(BF16)
                    er, ei = _cmul(c_re, c_im, pw_ref[p_v, d, pair, 0], pw_ref[p_v, d, pair, 1])
                    vt_scr[pair, r16, re_cols] = er.astype(BF16)
                    vt_scr[pair, r16, im_cols] = (-ei).astype(BF16)
                    er, ei = _cmul(c_re, c_im, pw_ref[p_z, d, pair, 0], pw_ref[p_z, d, pair, 1])
                    zt[d].append(jnp.concatenate([er, -ei], axis=1))
            bcat = [jnp.concatenate([bt_ref[d, g, 0], bt_ref[d, g, 1]], axis=1) for d in range(2)]
            f0 = _split_dot(bcat[0], jnp.concatenate(zt[0], axis=0)) + dt0_ref[g]
            b0 = _split_dot(bcat[1], jnp.concatenate(zt[1], axis=0))
            for s in range(c):
                fwd = jnp.where(slot >= s, _roll_lanes(f0, SSM_GROUP * s), 0.0)
                bwd = jnp.where(slot <= s, _roll_lanes(b0, SSM_GROUP * (s + 1)), 0.0)
                m_scr[g, s * SSM_GROUP:(s + 1) * SSM_GROUP, :] = (fwd + bwd).astype(BF16)

    ucat = jnp.concatenate(
        [u_ref[pl.ds(t, rows8, stride=SUBLANES), :].astype(BF16) for t in range(SUBLANES)], axis=1)
    grouped = jnp.dot(ucat, perm_ref[...], preferred_element_type=F32)
    for g in range(gpb):
        g_scr[g] = grouped[:, g * LANES:(g + 1) * LANES]
    for g in range(gpb):
        u_scr[g] = jnp.concatenate(
            [g_scr[g, pl.ds(k, rows, stride=sub), :] for k in range(sub)], axis=1).astype(BF16)

    for pair in range(npair):
        upair = jnp.concatenate([u_scr[2 * pair], u_scr[2 * pair + 1]], axis=1)
        sg = jnp.dot(upair, w_scr[pair], preferred_element_type=F32)
        for k in range(4):
            s_scr[:, k * sp + pair * LANES:k * sp + (pair + 1) * LANES] = sg[:, k * LANES:(k + 1) * LANES]

    nblk = nrows // SUBLANES
    row = lax.broadcasted_iota(jnp.int32, (SUBLANES, sp), 0)

    def scan_block(base, d, blk, carry):
        rows_ = pl.ds(pl.multiple_of(base + blk * SUBLANES, SUBLANES), SUBLANES)
        lo = d * 2 * sp
        pr = s_scr[rows_, lo:lo + sp]
        pi = s_scr[rows_, lo + sp:lo + 2 * sp]
        for step, sh in enumerate((1, 2, 4)):
            shift = sh if d == 0 else SUBLANES - sh
            tr, ti = _cmul(tab_ref[d, step, 0], tab_ref[d, step, 1],
                           pltpu.roll(pr, shift, 0), pltpu.roll(pi, shift, 0))
            pr = pr + tr
            pi = pi + ti
        cr, ci = carry
        tr, ti = _cmul(tab_ref[d, 3, 0], tab_ref[d, 3, 1], cr, ci)
        xr = pr + tr
        xi = pi + ti
        edge = 0 if d == 0 else SUBLANES - 1
        shift = 1 if d == 0 else SUBLANES - 1
        x_scr[rows_, lo:lo + sp] = jnp.where(row == edge, cr, pltpu.roll(xr, shift, 0))
        x_scr[rows_, lo + sp:lo + 2 * sp] = jnp.where(row == edge, ci, pltpu.roll(xi, shift, 0))
        last = SUBLANES - 1 if d == 0 else 0
        return (jnp.broadcast_to(xr[last:last + 1, :], (SUBLANES, sp)),
                jnp.broadcast_to(xi[last:last + 1, :], (SUBLANES, sp)))

    zero = jnp.zeros((SUBLANES, sp), F32)

    def body(i, carry):
        out = ()
        for sq in range(nseq):
            cf = scan_block(sq * nrows, 0, i, carry[4 * sq:4 * sq + 2])
            cb = scan_block(sq * nrows, 1, nblk - 1 - i, carry[4 * sq + 2:4 * sq + 4])
            out = out + cf + cb
        return out

    lax.fori_loop(0, nblk, body, (zero,) * (4 * nseq))

    for pair in range(npair):
        xpair = jnp.concatenate(
            [x_scr[:, k * sp + pair * LANES:k * sp + (pair + 1) * LANES] for k in range(4)],
            axis=1).astype(BF16)
        cross = lax.dot_general(xpair, vt_scr[pair], (((1,), (1,)), ((), ())),
                                preferred_element_type=F32)
        for gi in range(2):
            g = 2 * pair + gi
            yg = (jnp.dot(u_scr[g], m_scr[g], preferred_element_type=F32)
                  + cross[:, gi * width:(gi + 1) * width])
            for k in range(sub):
                g_scr[g, pl.ds(k, rows, stride=sub), :] = yg[:, k * LANES:(k + 1) * LANES]

    ycat = jnp.concatenate([g_scr[g] for g in range(gpb)], axis=1).astype(BF16)
    ytok = jnp.dot(ycat, perm_ref[...], preferred_element_type=F32)
    for t in range(SUBLANES):
        y_ref[pl.ds(t, rows8, stride=SUBLANES), :] = ytok[:, t * LANES:(t + 1) * LANES]


def _s5(u_blocks, mats, seq):
    perm, bt, ct, pw, dt0, tabs = mats
    nb, t, _ = u_blocks.shape
    c = SSM_CHUNK
    gpb = GROUPS_PER_BLOCK
    npair = gpb // 2
    nseq = SSM_SEQS
    while (t // seq) % nseq:
        nseq //= 2
    rows8 = nseq * seq // SUBLANES
    rows = nseq * seq // c
    width = c * SSM_GROUP
    sp = npair * LANES
    return pl.pallas_call(
        functools.partial(_s5_kernel, nseq=nseq, seq=seq),
        grid=(nb, t // (nseq * seq)),
        in_specs=[
            pl.BlockSpec((None, nseq * seq, LANES), lambda b, i: (b, i, 0)),
            _const_spec((SUBLANES * LANES, SUBLANES * LANES)),
            pl.BlockSpec((None, 2, gpb, 2, SSM_GROUP, LANES), lambda b, i: (b, 0, 0, 0, 0, 0)),
            pl.BlockSpec((None, 2, gpb, 2, SSM_GROUP, LANES), lambda b, i: (b, 0, 0, 0, 0, 0)),
            pl.BlockSpec((None, c + 1, 2, npair, 2, 1, LANES),
                         lambda b, i: (b, 0, 0, 0, 0, 0, 0)),
            pl.BlockSpec((None, gpb, SSM_GROUP, width), lambda b, i: (b, 0, 0, 0)),
            pl.BlockSpec((None, 2, 4, 2, SUBLANES, sp), lambda b, i: (b, 0, 0, 0, 0, 0)),
        ],
        out_specs=pl.BlockSpec((None, nseq * seq, LANES), lambda b, i: (b, i, 0)),
        out_shape=jax.ShapeDtypeStruct((nb, t, LANES), F32),
        scratch_shapes=[pltpu.VMEM((gpb, width, width), BF16),
                        pltpu.VMEM((npair, 2 * width, 4 * LANES), BF16),
                        pltpu.VMEM((npair, 2 * width, 4 * LANES), BF16),
                        pltpu.VMEM((gpb, rows8, LANES), F32),
                        pltpu.VMEM((gpb, rows, width), BF16),
                        pltpu.VMEM((rows, 4 * sp), F32),
                        pltpu.VMEM((rows, 4 * sp), F32)],
        compiler_params=pltpu.CompilerParams(
            dimension_semantics=("arbitrary", "arbitrary"), vmem_limit_bytes=VMEM_LIMIT),
        name="s5",
    )(u_blocks, perm, bt, ct, pw, dt0, tabs)


def _s5_tables(lam_re, lam_im, log_dt, b_re, b_im, c_re, c_im, d_skip):
    c = SSM_CHUNK
    nb, gpb = SSM_BLOCKS, GROUPS_PER_BLOCK
    npair = gpb // 2
    lr = jnp.minimum(lam_re.astype(F32), -1e-4)
    li = lam_im.astype(F32)
    dt = jnp.exp(log_dt.astype(F32))[..., None]
    p = jnp.arange(SUBLANES * c + 1, dtype=F32)[:, None, None, None]
    pmag = jnp.exp(p * (lr * dt)[None])
    pr = pmag * jnp.cos(p * (li * dt)[None])
    pi = pmag * jnp.sin(p * (li * dt)[None])
    ar, ai = pr[1], pi[1]
    den = lr * lr + li * li
    nr = ar - 1.0
    ni = ai
    cr = ((nr * lr + ni * li) / den)[:, :, None, :]
    ci = ((ni * lr - nr * li) / den)[:, :, None, :]
    b_re_t = b_re.astype(F32).transpose(0, 1, 3, 2)
    b_im_t = b_im.astype(F32).transpose(0, 1, 3, 2)
    bbar = jnp.stack([cr * b_re_t - ci * b_im_t, cr * b_im_t + ci * b_re_t], axis=2)
    cmat = jnp.stack([c_re.astype(F32), c_im.astype(F32)], axis=2)

    def half_placed(x):
        own = (jnp.arange(SSM_GROUPS) % 2)[:, None] == jnp.arange(2)[None, :]
        wide = jnp.where(own[None, :, None, None, :, None], x[:, :, :, :, None, :], 0.0)
        wide = wide.reshape(2, nb, gpb, 2, SSM_GROUP, LANES)
        return jnp.moveaxis(wide, 1, 0)

    pows = jnp.stack([pr, pi], axis=3)[:c + 1]
    pw = pows.reshape(c + 1, 2, nb, npair, 2, 2, SSM_STATE).transpose(2, 0, 1, 3, 5, 4, 6)
    pw = pw.reshape(nb, c + 1, 2, npair, 2, 1, LANES)

    lane = jnp.arange(c * SSM_GROUP)
    dt0 = jnp.where(lane[None, None, :] == jnp.arange(SSM_GROUP)[None, :, None],
                    d_skip.astype(F32).reshape(SSM_GROUPS, SSM_GROUP, 1), 0.0)
    dt0 = dt0.reshape(nb, gpb, SSM_GROUP, c * SSM_GROUP)

    rowi = jnp.arange(SUBLANES)

    def lanes(x):
        return x.reshape(x.shape[:-2] + (nb, npair * LANES))

    tabs = []
    for d in range(2):
        per_dir = []
        for sh in (1, 2, 4):
            keep = (rowi >= sh) if d == 0 else (rowi < SUBLANES - sh)
            per_dir.append(jnp.stack([
                jnp.where(keep[:, None, None], lanes(pr[sh * c, d])[None], 0.0),
                jnp.where(keep[:, None, None], lanes(pi[sh * c, d])[None], 0.0)]))
        cpow = (rowi + 1) * c if d == 0 else (SUBLANES - rowi) * c
        per_dir.append(jnp.stack([lanes(pr[cpow, d]), lanes(pi[cpow, d])]))
        tabs.append(jnp.stack(per_dir))
    tabs = jnp.stack(tabs).transpose(4, 0, 1, 2, 3, 5)

    src = jnp.arange(SUBLANES * LANES)
    dst = ((src >> 4) & 7) * LANES + (src >> 7) * SSM_GROUP + (src & 15)
    perm = (dst[:, None] == src[None, :]).astype(BF16)
    return perm, half_placed(bbar), half_placed(cmat), pw, dt0, tabs


def _mix_norm_kernel(x_ref, ret_ref, ssm_ref, glu_w_ref, glu_b_ref, w_out_ref, nffn_ref,
                     x1_ref, h_ref):
    ssm = jnp.concatenate([ssm_ref[b] for b in range(SSM_BLOCKS)], axis=1)
    y = jax.nn.gelu(ssm)
    gate = jax.nn.sigmoid(
        jnp.dot(y.astype(BF16), glu_w_ref[...], preferred_element_type=F32) + glu_b_ref[...])
    mix = jnp.concatenate([ret_ref[...], (y * gate).astype(BF16)], axis=1)
    x1 = x_ref[...] + jnp.dot(mix, w_out_ref[...], preferred_element_type=F32)
    x1_ref[...] = x1
    h_ref[...] = _rms(x1, nffn_ref[...]).astype(BF16)


def _mix_norm(x2, ret, ssm, glu_w, glu_b, w_out, norm_ffn):
    t = x2.shape[0]
    r = MIX_TILE
    tok = lambda i: (i, 0)
    return pl.pallas_call(
        _mix_norm_kernel,
        grid=(t // r,),
        in_specs=[pl.BlockSpec((r, D_MODEL), tok), pl.BlockSpec((r, RET_WIDTH), tok),
                  pl.BlockSpec((SSM_BLOCKS, r, LANES), lambda i: (0, i, 0)),
                  _const_spec((SSM_WIDTH, SSM_WIDTH)), _const_spec((1, SSM_WIDTH)),
                  _const_spec((D_MODEL, D_MODEL)), _const_spec((1, D_MODEL))],
        out_specs=[pl.BlockSpec((r, D_MODEL), tok), pl.BlockSpec((r, D_MODEL), tok)],
        out_shape=[jax.ShapeDtypeStruct((t, D_MODEL), F32),
                   jax.ShapeDtypeStruct((t, D_MODEL), BF16)],
        compiler_params=pltpu.CompilerParams(
            dimension_semantics=("arbitrary",), vmem_limit_bytes=VMEM_LIMIT),
        name="mix_norm",
    )(x2, ret, ssm, glu_w, glu_b, w_out, norm_ffn)


def _conv_ffn_kernel(x1_ref, hm_ref, hp_ref, hn_ref, w_up_ref, cw_ref, cb_ref, w_down_ref,
                     nfin_ref, o_ref, hid_scr, *, tiles_per_seq):
    r = hm_ref.shape[0]
    ext = r + 2 * HALO
    j = pl.program_id(0) % tiles_per_seq
    hp = jnp.where(j == 0, jnp.zeros(hp_ref.shape, BF16), hp_ref[...])
    hn = jnp.where(j == tiles_per_seq - 1, jnp.zeros(hn_ref.shape, BF16), hn_ref[...])
    h = jnp.concatenate([hp, hm_ref[...], hn], axis=0)
    for jj in range(D_FF // FF_TILE):
        parts = []
        for off in (jj * FF_TILE, D_FF + jj * FF_TILE):
            z = jnp.dot(h, w_up_ref[:, off:off + FF_TILE], preferred_element_type=F32)
            zc = (pltpu.roll(z, 1, 0)[HALO:HALO + r] * cw_ref[0:1, off:off + FF_TILE]
                  + z[HALO:HALO + r] * cw_ref[1:2, off:off + FF_TILE]
                  + pltpu.roll(z, ext - 1, 0)[HALO:HALO + r] * cw_ref[2:3, off:off + FF_TILE]
                  + cb_ref[:, off:off + FF_TILE])
            parts.append(zc)
        hid_scr[:, jj * FF_TILE:(jj + 1) * FF_TILE] = (jax.nn.gelu(parts[1]) * parts[0]).astype(BF16)
    x2 = x1_ref[...] + jnp.dot(hid_scr[...], w_down_ref[...], preferred_element_type=F32)
    o_ref[...] = _rms(x2, nfin_ref[...])


def _conv_ffn(x1, h, w_up, conv_w, conv_b, w_down, norm_final, seq):
    t = x1.shape[0]
    r = TOK_TILE
    tiles_per_seq = seq // r
    hb = r // HALO
    last = t // HALO - 1
    tok = lambda i: (i, 0)
    return pl.pallas_call(
        functools.partial(_conv_ffn_kernel, tiles_per_seq=tiles_per_seq),
        grid=(t // r,),
        in_specs=[pl.BlockSpec((r, D_MODEL), tok), pl.BlockSpec((r, D_MODEL), tok),
                  pl.BlockSpec((HALO, D_MODEL), lambda i: (jnp.maximum(i * hb - 1, 0), 0)),
                  pl.BlockSpec((HALO, D_MODEL), lambda i: (jnp.minimum((i + 1) * hb, last), 0)),
                  _const_spec((D_MODEL, 2 * D_FF)), _const_spec((3, 2 * D_FF)),
                  _const_spec((1, 2 * D_FF)), _const_spec((D_FF, D_MODEL)),
                  _const_spec((1, D_MODEL))],
        out_specs=pl.BlockSpec((r, D_MODEL), tok),
        out_shape=jax.ShapeDtypeStruct((t, D_MODEL), F32),
        scratch_shapes=[pltpu.VMEM((r, D_FF), BF16)],
        compiler_params=pltpu.CompilerParams(
            dimension_semantics=("arbitrary",), vmem_limit_bytes=VMEM_LIMIT),
        name="conv_ffn",
    )(x1, h, h, h, w_up, conv_w, conv_b, w_down, norm_final)


def _rotary_tables(seq):
    half = HEAD_DIM // 2
    inv_freq = ROPE_BASE ** (-jnp.arange(half, dtype=F32) / half)
    ang = jnp.arange(seq, dtype=F32)[:, None] * inv_freq[None, :]
    cos = jnp.cos(ang)
    sin = jnp.sin(ang)
    return jnp.concatenate([cos, cos], axis=1), jnp.concatenate([-sin, sin], axis=1)


def _layer(x, prep):
    b, seq, _ = x.shape
    x2 = x.reshape(b * seq, D_MODEL)
    q, k, v, g, u = _in_proj(x2, prep["norm_mix"], prep["w_in"], prep["cos"], prep["sin"], seq)
    ret = _retention(q, k, v, g, prep["ret_tabs"], prep["gn_gain"], seq)
    ssm = _s5(u, prep["s5"], seq)
    x1, h = _mix_norm(x2, ret, ssm, prep["glu_w"], prep["glu_b"], prep["w_out"], prep["norm_ffn"])
    out = _conv_ffn(x1, h, prep["w_up"], prep["conv_w"], prep["conv_b"], prep["w_down"],
                    prep["norm_final"], seq)
    return out.reshape(b, seq, D_MODEL)


def kernel(x_prompt, x_sample, norm_mix, w_in, ret_gn_gain, s5_lambda_re, s5_lambda_im, s5_log_dt, s5_B_re, s5_B_im, s5_C_re, s5_C_im, s5_D, s5_glu_w, s5_glu_b, w_out, norm_ffn, w_up, conv_w, conv_b, w_down, norm_final):
    assert norm_mix.shape[0] == 1, "single-layer trunk"
    seq = x_prompt.shape[1]
    assert x_sample.shape[1] == seq
    cos, sin = _rotary_tables(seq)
    prep = {
        "norm_mix": norm_mix[0][None].astype(F32),
        "w_in": w_in[0].astype(BF16),
        "cos": cos, "sin": sin,
        "ret_tabs": _retention_tables(),
        "gn_gain": ret_gn_gain[0][None].astype(F32),
        "s5": _s5_tables(s5_lambda_re[0], s5_lambda_im[0], s5_log_dt[0], s5_B_re[0], s5_B_im[0],
                         s5_C_re[0], s5_C_im[0], s5_D[0]),
        "glu_w": s5_glu_w[0].astype(BF16),
        "glu_b": s5_glu_b[0][None].astype(F32),
        "w_out": w_out[0].astype(BF16),
        "norm_ffn": norm_ffn[0][None].astype(F32),
        "w_up": w_up[0].astype(BF16),
        "conv_w": conv_w[0].astype(F32),
        "conv_b": conv_b[0][None].astype(F32),
        "w_down": w_down[0].astype(BF16),
        "norm_final": norm_final[None].astype(F32),
    }
    return (_layer(x_prompt, prep), _layer(x_sample, prep))
```

```python
import functools

import jax
import jax.numpy as jnp
from jax import lax
from jax.experimental import pallas as pl
from jax.experimental.pallas import tpu as pltpu

D_MODEL = 1024
RET_WIDTH = 512
SSM_WIDTH = 512
RET_HEADS = 4
HEAD_DIM = 128
ROPE_BASE = 10000.0
SSM_GROUP = 16
SSM_GROUPS = 32
SSM_STATE = 64
D_FF = 2816
EPS = 1e-6
IN_WIDTH = 4 * RET_WIDTH + SSM_WIDTH

LANES = 128
SUBLANES = 8
BF16_ROWS = 16
VMEM_LIMIT = 56 * 1024 * 1024

RET_BLOCK = 256
SSM_BLOCKS = SSM_WIDTH // LANES
GROUPS_PER_BLOCK = LANES // SSM_GROUP
SSM_CHUNK = 16
SSM_SEQS = 4
TOK_TILE = 512
MIX_TILE = 1024
FF_TILE = 256
HALO = BF16_ROWS

BF16 = jnp.bfloat16
F32 = jnp.float32


def _rms(x, w):
    return x * lax.rsqrt(jnp.mean(x * x, axis=-1, keepdims=True) + EPS) * w


def _const_spec(shape):
    nd = len(shape)
    return pl.BlockSpec(shape, lambda *_: (0,) * nd, pipeline_mode=pl.Buffered(1))


def _in_proj_kernel(x_ref, nw_ref, w_ref, cos_ref, sin_ref, gain_ref,
                    q_ref, k_ref, v_ref, g_ref, u_ref):
    h = _rms(x_ref[...], nw_ref[...]).astype(BF16)
    proj = jnp.dot(h, w_ref[...], preferred_element_type=F32)
    cos = cos_ref[...]
    sin = sin_ref[...]
    kscale = HEAD_DIM ** -0.5
    for hd in range(RET_HEADS):
        lo = hd * HEAD_DIM
        qh = proj[:, lo:lo + HEAD_DIM]
        q_ref[:, lo:lo + HEAD_DIM] = (qh * cos + pltpu.roll(qh, HEAD_DIM // 2, 1) * sin).astype(BF16)
        kh = proj[:, RET_WIDTH + lo:RET_WIDTH + lo + HEAD_DIM]
        kr = kh * cos + pltpu.roll(kh, HEAD_DIM // 2, 1) * sin
        k_ref[:, lo:lo + HEAD_DIM] = (kr * kscale).astype(BF16)
    v_ref[...] = proj[:, 2 * RET_WIDTH:3 * RET_WIDTH].astype(BF16)
    g_ref[...] = (jax.nn.silu(proj[:, 3 * RET_WIDTH:4 * RET_WIDTH]) * gain_ref[...]).astype(BF16)
    for blk in range(SSM_BLOCKS):
        lo = 4 * RET_WIDTH + blk * LANES
        u_ref[blk] = proj[:, lo:lo + LANES]


def _in_proj(x2, norm_w, w_in, cos2, sin2, gain, seq):
    t = x2.shape[0]
    r = TOK_TILE
    tiles_per_seq = seq // r
    tok = lambda i: (i, 0)
    out_tok = jax.ShapeDtypeStruct((t, RET_WIDTH), BF16)
    return pl.pallas_call(
        _in_proj_kernel,
        grid=(t // r,),
        in_specs=[
            pl.BlockSpec((r, D_MODEL), tok),
            _const_spec((1, D_MODEL)),
            _const_spec((D_MODEL, IN_WIDTH)),
            pl.BlockSpec((r, HEAD_DIM), lambda i: (i % tiles_per_seq, 0)),
            pl.BlockSpec((r, HEAD_DIM), lambda i: (i % tiles_per_seq, 0)),
            _const_spec((1, RET_WIDTH)),
        ],
        out_specs=[
            pl.BlockSpec((r, RET_WIDTH), tok),
            pl.BlockSpec((r, RET_WIDTH), tok),
            pl.BlockSpec((r, RET_WIDTH), tok),
            pl.BlockSpec((r, RET_WIDTH), tok),
            pl.BlockSpec((SSM_BLOCKS, r, LANES), lambda i: (0, i, 0)),
        ],
        out_shape=[out_tok, out_tok, out_tok, out_tok,
                   jax.ShapeDtypeStruct((SSM_BLOCKS, t, LANES), F32)],
        compiler_params=pltpu.CompilerParams(
            dimension_semantics=("arbitrary",), vmem_limit_bytes=VMEM_LIMIT),
        name="in_proj",
    )(x2, norm_w, w_in, cos2, sin2, gain)


def _retention_kernel(q_ref, k_ref, v_ref, g_ref, dmat_ref, cd_ref, row_ref, o_ref,
                      kv_scr, r_scr):
    c = RET_BLOCK
    nchunks = q_ref.shape[0] // c
    heads = [slice(hd * HEAD_DIM, (hd + 1) * HEAD_DIM) for hd in range(RET_HEADS)]

    def summarize(n, carry):
        rows = pl.ds(pl.multiple_of(n * c, c), c)
        for hd, cols in enumerate(heads):
            kn = k_ref[rows, cols]
            kcat = jnp.concatenate([kn * row_ref[hd, 0], kn * row_ref[hd, 1]], axis=1)
            kv_scr[n, hd] = lax.dot_general(kcat, v_ref[rows, cols], (((0,), (0,)), ((), ())),
                                            preferred_element_type=F32)
        return carry

    lax.fori_loop(0, nchunks, summarize, 0, unroll=2)

    for hd in range(RET_HEADS):
        cd = cd_ref[hd]
        rf = jnp.zeros((HEAD_DIM, HEAD_DIM), F32)
        for n in range(nchunks):
            r_scr[n, hd, 0:HEAD_DIM, :] = rf.astype(BF16)
            rf = cd * rf + kv_scr[n, hd, 0:HEAD_DIM, :]
        rb = jnp.zeros((HEAD_DIM, HEAD_DIM), F32)
        for n in reversed(range(nchunks)):
            r_scr[n, hd, HEAD_DIM:2 * HEAD_DIM, :] = rb.astype(BF16)
            rb = cd * rb + kv_scr[n, hd, HEAD_DIM:2 * HEAD_DIM, :]

    def emit(n, carry):
        rows = pl.ds(pl.multiple_of(n * c, c), c)
        for hd, cols in enumerate(heads):
            qn = q_ref[rows, cols]
            s = lax.dot_general(qn, k_ref[rows, cols], (((1,), (1,)), ((), ())),
                                preferred_element_type=F32) * dmat_ref[hd]
            inner = jnp.dot(s.astype(BF16), v_ref[rows, cols], preferred_element_type=F32)
            qcat = jnp.concatenate([qn * row_ref[hd, 2], qn * row_ref[hd, 3]], axis=1)
            o = inner + jnp.dot(qcat, r_scr[n, hd], preferred_element_type=F32)
            o = o * lax.rsqrt(jnp.mean(o * o, axis=-1, keepdims=True) + EPS)
            o_ref[rows, cols] = (g_ref[rows, cols].astype(F32) * o).astype(BF16)
        return carry

    lax.fori_loop(0, nchunks, emit, 0, unroll=4)


def _retention(q, k, v, g, tabs, seq):
    dmat, cd, tab_rows = tabs
    t = q.shape[0]
    nchunks = seq // RET_BLOCK
    blk = pl.BlockSpec((seq, RET_WIDTH), lambda b: (b, 0))
    return pl.pallas_call(
        _retention_kernel,
        grid=(t // seq,),
        in_specs=[blk, blk, blk, blk,
                  _const_spec((RET_HEADS, RET_BLOCK, RET_BLOCK)),
                  _const_spec((RET_HEADS, HEAD_DIM, HEAD_DIM)),
                  _const_spec((RET_HEADS, 4, RET_BLOCK, HEAD_DIM))],
        out_specs=blk,
        out_shape=jax.ShapeDtypeStruct((t, RET_WIDTH), BF16),
        scratch_shapes=[pltpu.VMEM((nchunks, RET_HEADS, 2 * HEAD_DIM, HEAD_DIM), F32),
                        pltpu.VMEM((nchunks, RET_HEADS, 2 * HEAD_DIM, HEAD_DIM), BF16)],
        compiler_params=pltpu.CompilerParams(
            dimension_semantics=("arbitrary",), vmem_limit_bytes=VMEM_LIMIT),
        name="retention",
    )(q, k, v, g, dmat, cd, tab_rows)


def _retention_tables():
    c = RET_BLOCK
    lg = jnp.log(1.0 - 2.0 ** (-5.0 - jnp.arange(RET_HEADS, dtype=F32)))[:, None, None]
    idx = jnp.arange(c, dtype=F32)
    col = jnp.broadcast_to(idx[:, None], (c, HEAD_DIM))[None]
    dmat = jnp.exp(lg * jnp.abs(idx[:, None] - idx[None, :])[None])
    cd = jnp.exp(lg * c) * jnp.ones((1, HEAD_DIM, HEAD_DIM), F32)
    zf = jnp.exp(lg * (c - 1.0 - col))
    zb = jnp.exp(lg * col)
    xf = jnp.exp(lg * (col + 1.0))
    xb = jnp.exp(lg * (c - col))
    return dmat, cd, jnp.stack([zf, zb, xf, xb], axis=1).astype(BF16)


def _cmul(ar, ai, br, bi):
    return ar * br - ai * bi, ar * bi + ai * br


def _roll_lanes(x, shift):
    ntiles = x.shape[1] // LANES
    whole, part = divmod(shift % x.shape[1], LANES)
    tiles = [x[:, k * LANES:(k + 1) * LANES] for k in range(ntiles)]
    tiles = [tiles[(k - whole) % ntiles] for k in range(ntiles)]
    if part:
        lane = lax.broadcasted_iota(jnp.int32, tiles[0].shape, 1)
        rolled = [pltpu.roll(t, part, 1) for t in tiles]
        tiles = [jnp.where(lane < part, rolled[(k - 1) % ntiles], rolled[k]) for k in range(ntiles)]
    return tiles[0] if ntiles == 1 else jnp.concatenate(tiles, axis=1)


def _split_dot(a, z):
    dims = (((1,), (1,)), ((), ()))
    a_hi = a.astype(BF16)
    z_hi = z.astype(BF16)
    a_lo = (a - a_hi.astype(F32)).astype(BF16)
    z_lo = (z - z_hi.astype(F32)).astype(BF16)
    return (lax.dot_general(a_hi, z_hi, dims, preferred_element_type=F32)
            + lax.dot_general(a_hi, z_lo, dims, preferred_element_type=F32)
            + lax.dot_general(a_lo, z_hi, dims, preferred_element_type=F32))


def _s5_kernel(u_ref, perm_ref, bt_ref, ct_ref, pw_ref, dt0_ref, tab_ref, y_ref,
               m_scr, w_scr, vt_scr, g_scr, u_scr, s_scr, x_scr, *, nseq, seq):
    c = SSM_CHUNK
    sub = c // SUBLANES
    gpb = GROUPS_PER_BLOCK
    npair = gpb // 2
    width = c * SSM_GROUP
    sp = npair * LANES
    rows8 = nseq * seq // SUBLANES
    rows = rows8 // sub
    nrows = seq // c

    @pl.when(pl.program_id(1) == 0)
    def _expand():
        slot = lax.broadcasted_iota(jnp.int32, (SSM_GROUP, width), 1) >> 4
        for g in range(gpb):
            pair, gi = divmod(g, 2)
            zt = ([], [])
            for d in range(2):
                b_re, b_im = bt_ref[d, g, 0], bt_ref[d, g, 1]
                c_re, c_im = ct_ref[d, g, 0], ct_ref[d, g, 1]
                for s in range(c):
                    r16 = slice(gi * width + s * SSM_GROUP, gi * width + (s + 1) * SSM_GROUP)
                    re_cols = slice((2 * d) * LANES, (2 * d + 1) * LANES)
                    im_cols = slice((2 * d + 1) * LANES, (2 * d + 2) * LANES)
                    p_w = c - 1 - s if d == 0 else s
                    p_v = s + 1 if d == 0 else c - s
                    p_z = s if d == 0 else c - 1 - s
                    wr, wi = _cmul(b_re, b_im, pw_ref[p_w, d, pair, 0], pw_ref[p_w, d, pair, 1])
                    w_scr[pair, r16, re_cols] = wr.astype(BF16)
                    w_scr[pair, r16, im_cols] = wi.astype(BF16)
                    er, ei = _cmul(c_re, c_im, pw_ref[p_v, d, pair, 0], pw_ref[p_v, d, pair, 1])
                    vt_scr[pair, r16, re_cols] = er.astype(BF16)
                    vt_scr[pair, r16, im_cols] = (-ei).astype(BF16)
                    er, ei = _cmul(c_re, c_im, pw_ref[p_z, d, pair, 0], pw_ref[p_z, d, pair, 1])
                    zt[d].append(jnp.concatenate([er, -ei], axis=1))
            bcat = [jnp.concatenate([bt_ref[d, g, 0], bt_ref[d, g, 1]], axis=1) for d in range(2)]
            f0 = _split_dot(bcat[0], jnp.concatenate(zt[0], axis=0)) + dt0_ref[g]
            b0 = _split_dot(bcat[1], jnp.concatenate(zt[1], axis=0))
            for s in range(c):
                fwd = jnp.where(slot >= s, _roll_lanes(f0, SSM_GROUP * s), 0.0)
                bwd = jnp.where(slot <= s, _roll_lanes(b0, SSM_GROUP * (s + 1)), 0.0)
                m_scr[g, s * SSM_GROUP:(s + 1) * SSM_GROUP, :] = (fwd + bwd).astype(BF16)

    ucat = jnp.concatenate(
        [u_ref[pl.ds(t, rows8, stride=SUBLANES), :].astype(BF16) for t in range(SUBLANES)], axis=1)
    grouped = jnp.dot(ucat, perm_ref[...], preferred_element_type=F32)
    for g in range(gpb):
        g_scr[g] = grouped[:, g * LANES:(g + 1) * LANES]
    for g in range(gpb):
        u_scr[g] = jnp.concatenate(
            [g_scr[g, pl.ds(k, rows, stride=sub), :] for k in range(sub)], axis=1).astype(BF16)

    for pair in range(npair):
        upair = jnp.concatenate([u_scr[2 * pair], u_scr[2 * pair + 1]], axis=1)
        sg = jnp.dot(upair, w_scr[pair], preferred_element_type=F32)
        for k in range(4):
            s_scr[:, k * sp + pair * LANES:k * sp + (pair + 1) * LANES] = sg[:, k * LANES:(k + 1) * LANES]

    nblk = nrows // SUBLANES
    row = lax.broadcasted_iota(jnp.int32, (SUBLANES, sp), 0)

    def scan_block(base, d, blk, carry):
        rows_ = pl.ds(pl.multiple_of(base + blk * SUBLANES, SUBLANES), SUBLANES)
        lo = d * 2 * sp
        pr = s_scr[rows_, lo:lo + sp]
        pi = s_scr[rows_, lo + sp:lo + 2 * sp]
        for step, sh in enumerate((1, 2, 4)):
            shift = sh if d == 0 else SUBLANES - sh
            tr, ti = _cmul(tab_ref[d, step, 0], tab_ref[d, step, 1],
                           pltpu.roll(pr, shift, 0), pltpu.roll(pi, shift, 0))
            pr = pr + tr
            pi = pi + ti
        cr, ci = carry
        tr, ti = _cmul(tab_ref[d, 3, 0], tab_ref[d, 3, 1], cr, ci)
        xr = pr + tr
        xi = pi + ti
        edge = 0 if d == 0 else SUBLANES - 1
        shift = 1 if d == 0 else SUBLANES - 1
        x_scr[rows_, lo:lo + sp] = jnp.where(row == edge, cr, pltpu.roll(xr, shift, 0))
        x_scr[rows_, lo + sp:lo + 2 * sp] = jnp.where(row == edge, ci, pltpu.roll(xi, shift, 0))
        last = SUBLANES - 1 if d == 0 else 0
        return (jnp.broadcast_to(xr[last:last + 1, :], (SUBLANES, sp)),
                jnp.broadcast_to(xi[last:last + 1, :], (SUBLANES, sp)))

    zero = jnp.zeros((SUBLANES, sp), F32)

    def body(i, carry):
        out = ()
        for sq in range(nseq):
            cf = scan_block(sq * nrows, 0, i, carry[4 * sq:4 * sq + 2])
            cb = scan_block(sq * nrows, 1, nblk - 1 - i, carry[4 * sq + 2:4 * sq + 4])
            out = out + cf + cb
        return out

    lax.fori_loop(0, nblk, body, (zero,) * (4 * nseq))

    for pair in range(npair):
        xpair = jnp.concatenate(
            [x_scr[:, k * sp + pair * LANES:k * sp + (pair + 1) * LANES] for k in range(4)],
            axis=1).astype(BF16)
        cross = lax.dot_general(xpair, vt_scr[pair], (((1,), (1,)), ((), ())),
                                preferred_element_type=F32)
        for gi in range(2):
            g = 2 * pair + gi
            yg = (jnp.dot(u_scr[g], m_scr[g], preferred_element_type=F32)
                  + cross[:, gi * width:(gi + 1) * width])
            for k in range(sub):
                g_scr[g, pl.ds(k, rows, stride=sub), :] = yg[:, k * LANES:(k + 1) * LANES]

    ycat = jnp.concatenate([g_scr[g] for g in range(gpb)], axis=1).astype(BF16)
    ytok = jnp.dot(ycat, perm_ref[...], preferred_element_type=F32)
    for t in range(SUBLANES):
        y_ref[pl.ds(t, rows8, stride=SUBLANES), :] = ytok[:, t * LANES:(t + 1) * LANES]


def _s5(u_blocks, mats, seq):
    perm, bt, ct, pw, dt0, tabs = mats
    nb, t, _ = u_blocks.shape
    c = SSM_CHUNK
    gpb = GROUPS_PER_BLOCK
    npair = gpb // 2
    nseq = SSM_SEQS
    while (t // seq) % nseq:
        nseq //= 2
    rows8 = nseq * seq // SUBLANES
    rows = nseq * seq // c
    width = c * SSM_GROUP
    sp = npair * LANES
    return pl.pallas_call(
        functools.partial(_s5_kernel, nseq=nseq, seq=seq),
        grid=(nb, t // (nseq * seq)),
        in_specs=[
            pl.BlockSpec((None, nseq * seq, LANES), lambda b, i: (b, i, 0)),
            _const_spec((SUBLANES * LANES, SUBLANES * LANES)),
            pl.BlockSpec((None, 2, gpb, 2, SSM_GROUP, LANES), lambda b, i: (b, 0, 0, 0, 0, 0)),
            pl.BlockSpec((None, 2, gpb, 2, SSM_GROUP, LANES), lambda b, i: (b, 0, 0, 0, 0, 0)),
            pl.BlockSpec((None, c + 1, 2, npair, 2, 1, LANES),
                         lambda b, i: (b, 0, 0, 0, 0, 0, 0)),
            pl.BlockSpec((None, gpb, SSM_GROUP, width), lambda b, i: (b, 0, 0, 0)),
            pl.BlockSpec((None, 2, 4, 2, SUBLANES, sp), lambda b, i: (b, 0, 0, 0, 0, 0)),
        ],
        out_specs=pl.BlockSpec((None, nseq * seq, LANES), lambda b, i: (b, i, 0)),
        out_shape=jax.ShapeDtypeStruct((nb, t, LANES), F32),
        scratch_shapes=[pltpu.VMEM((gpb, width, width), BF16),
                        pltpu.VMEM((npair, 2 * width, 4 * LANES), BF16),
                        pltpu.VMEM((npair, 2 * width, 4 * LANES), BF16),
                        pltpu.VMEM((gpb, rows8, LANES), F32),
                        pltpu.VMEM((gpb, rows, width), BF16),
                        pltpu.VMEM((rows, 4 * sp), F32),
                        pltpu.VMEM((rows, 4 * sp), F32)],
        compiler_params=pltpu.CompilerParams(
            dimension_semantics=("arbitrary", "arbitrary"), vmem_limit_bytes=VMEM_LIMIT),
        name="s5",
    )(u_blocks, perm, bt, ct, pw, dt0, tabs)


def _s5_tables(lam_re, lam_im, log_dt, b_re, b_im, c_re, c_im, d_skip):
    c = SSM_CHUNK
    nb, gpb = SSM_BLOCKS, GROUPS_PER_BLOCK
    npair = gpb // 2
    lr = jnp.minimum(lam_re.astype(F32), -1e-4)
    li = lam_im.astype(F32)
    dt = jnp.exp(log_dt.astype(F32))[..., None]
    plist = list(range(c + 1)) + [c * k for k in range(2, SUBLANES + 1)]
    pidx = {pw_: i for i, pw_ in enumerate(plist)}
    p = jnp.asarray(plist, F32)[:, None, None, None]
    pmag = jnp.exp(p * (lr * dt)[None])
    pr = pmag * jnp.cos(p * (li * dt)[None])
    pi = pmag * jnp.sin(p * (li * dt)[None])
    ar, ai = pr[1], pi[1]
    den = lr * lr + li * li
    nr = ar - 1.0
    ni = ai
    cr = ((nr * lr + ni * li) / den)[:, :, None, :]
    ci = ((ni * lr - nr * li) / den)[:, :, None, :]
    b_re_t = b_re.astype(F32).transpose(0, 1, 3, 2)
    b_im_t = b_im.astype(F32).transpose(0, 1, 3, 2)
    bbar = jnp.stack([cr * b_re_t - ci * b_im_t, cr * b_im_t + ci * b_re_t], axis=2)
    cmat = jnp.stack([c_re.astype(F32), c_im.astype(F32)], axis=2)

    def half_placed(x):
        own = (jnp.arange(SSM_GROUPS) % 2)[:, None] == jnp.arange(2)[None, :]
        wide = jnp.where(own[None, :, None, None, :, None], x[:, :, :, :, None, :], 0.0)
        wide = wide.reshape(2, nb, gpb, 2, SSM_GROUP, LANES)
        return jnp.moveaxis(wide, 1, 0)

    pows = jnp.stack([pr, pi], axis=3)[:c + 1]
    pw = pows.reshape(c + 1, 2, nb, npair, 2, 2, SSM_STATE).transpose(2, 0, 1, 3, 5, 4, 6)
    pw = pw.reshape(nb, c + 1, 2, npair, 2, 1, LANES)

    lane = jnp.arange(c * SSM_GROUP)
    dt0 = jnp.where(lane[None, None, :] == jnp.arange(SSM_GROUP)[None, :, None],
                    d_skip.astype(F32).reshape(SSM_GROUPS, SSM_GROUP, 1), 0.0)
    dt0 = dt0.reshape(nb, gpb, SSM_GROUP, c * SSM_GROUP)

    rowi = jnp.arange(SUBLANES)

    def lanes(x):
        return x.reshape(x.shape[:-2] + (nb, npair * LANES))

    tabs = []
    for d in range(2):
        per_dir = []
        for sh in (1, 2, 4):
            keep = (rowi >= sh) if d == 0 else (rowi < SUBLANES - sh)
            per_dir.append(jnp.stack([
                jnp.where(keep[:, None, None], lanes(pr[pidx[sh * c], d])[None], 0.0),
                jnp.where(keep[:, None, None], lanes(pi[pidx[sh * c], d])[None], 0.0)]))
        mult = range(1, SUBLANES + 1) if d == 0 else range(SUBLANES, 0, -1)
        cpow = jnp.asarray([pidx[k * c] for k in mult])
        per_dir.append(jnp.stack([lanes(pr[cpow, d]), lanes(pi[cpow, d])]))
        tabs.append(jnp.stack(per_dir))
    tabs = jnp.stack(tabs).transpose(4, 0, 1, 2, 3, 5)

    src = jnp.arange(SUBLANES * LANES)
    dst = ((src >> 4) & 7) * LANES + (src >> 7) * SSM_GROUP + (src & 15)
    perm = (dst[:, None] == src[None, :]).astype(BF16)
    return perm, half_placed(bbar), half_placed(cmat), pw, dt0, tabs


def _mix_norm_kernel(x_ref, ret_ref, ssm_ref, glu_w_ref, glu_b_ref, w_out_ref, nffn_ref,
                     mix_ref, h_ref):
    ssm = jnp.concatenate([ssm_ref[b] for b in range(SSM_BLOCKS)], axis=1)
    y = jax.nn.gelu(ssm)
    gate = jax.nn.sigmoid(
        jnp.dot(y.astype(BF16), glu_w_ref[...], preferred_element_type=F32) + glu_b_ref[...])
    cat = jnp.concatenate([ret_ref[...], (y * gate).astype(BF16)], axis=1)
    mix = jnp.dot(cat, w_out_ref[...], preferred_element_type=F32).astype(BF16)
    mix_ref[...] = mix
    h_ref[...] = _rms(x_ref[...] + mix.astype(F32), nffn_ref[...]).astype(BF16)


def _mix_norm(x2, ret, ssm, glu_w, glu_b, w_out, norm_ffn):
    t = x2.shape[0]
    r = MIX_TILE
    tok = lambda i: (i, 0)
    return pl.pallas_call(
        _mix_norm_kernel,
        grid=(t // r,),
        in_specs=[pl.BlockSpec((r, D_MODEL), tok), pl.BlockSpec((r, RET_WIDTH), tok),
                  pl.BlockSpec((SSM_BLOCKS, r, LANES), lambda i: (0, i, 0)),
                  _const_spec((SSM_WIDTH, SSM_WIDTH)), _const_spec((1, SSM_WIDTH)),
                  _const_spec((D_MODEL, D_MODEL)), _const_spec((1, D_MODEL))],
        out_specs=[pl.BlockSpec((r, D_MODEL), tok), pl.BlockSpec((r, D_MODEL), tok)],
        out_shape=[jax.ShapeDtypeStruct((t, D_MODEL), BF16),
                   jax.ShapeDtypeStruct((t, D_MODEL), BF16)],
        compiler_params=pltpu.CompilerParams(
            dimension_semantics=("arbitrary",), vmem_limit_bytes=VMEM_LIMIT),
        name="mix_norm",
    )(x2, ret, ssm, glu_w, glu_b, w_out, norm_ffn)


def _conv_ffn_kernel(x_ref, mix_ref, hm_ref, hp_ref, hn_ref, w_up_ref, cw_ref, cb_ref,
                     w_down_ref, nfin_ref, o_ref, hid_scr, *, tiles_per_seq):
    r = hm_ref.shape[0]
    ext = r + 2 * HALO
    j = pl.program_id(0) % tiles_per_seq
    hp = jnp.where(j == 0, jnp.zeros(hp_ref.shape, BF16), hp_ref[...])
    hn = jnp.where(j == tiles_per_seq - 1, jnp.zeros(hn_ref.shape, BF16), hn_ref[...])
    h = jnp.concatenate([hp, hm_ref[...], hn], axis=0)
    for jj in range(D_FF // FF_TILE):
        parts = []
        for off in (jj * FF_TILE, D_FF + jj * FF_TILE):
            z = jnp.dot(h, w_up_ref[:, off:off + FF_TILE], preferred_element_type=F32)
            zc = (pltpu.roll(z, 1, 0)[HALO:HALO + r] * cw_ref[0:1, off:off + FF_TILE]
                  + z[HALO:HALO + r] * cw_ref[1:2, off:off + FF_TILE]
                  + pltpu.roll(z, ext - 1, 0)[HALO:HALO + r] * cw_ref[2:3, off:off + FF_TILE]
                  + cb_ref[:, off:off + FF_TILE])
            parts.append(zc)
        hid_scr[:, jj * FF_TILE:(jj + 1) * FF_TILE] = (jax.nn.gelu(parts[1]) * parts[0]).astype(BF16)
    x1 = x_ref[...] + mix_ref[...].astype(F32)
    x2 = x1 + jnp.dot(hid_scr[...], w_down_ref[...], preferred_element_type=F32)
    o_ref[...] = _rms(x2, nfin_ref[...])


def _conv_ffn(x, mix, h, w_up, conv_w, conv_b, w_down, norm_final, seq):
    t = x.shape[0]
    r = TOK_TILE
    tiles_per_seq = seq // r
    hb = r // HALO
    last = t // HALO - 1
    tok = lambda i: (i, 0)
    return pl.pallas_call(
        functools.partial(_conv_ffn_kernel, tiles_per_seq=tiles_per_seq),
        grid=(t // r,),
        in_specs=[pl.BlockSpec((r, D_MODEL), tok), pl.BlockSpec((r, D_MODEL), tok),
                  pl.BlockSpec((r, D_MODEL), tok),
                  pl.BlockSpec((HALO, D_MODEL), lambda i: (jnp.maximum(i * hb - 1, 0), 0)),
                  pl.BlockSpec((HALO, D_MODEL), lambda i: (jnp.minimum((i + 1) * hb, last), 0)),
                  _const_spec((D_MODEL, 2 * D_FF)), _const_spec((3, 2 * D_FF)),
                  _const_spec((1, 2 * D_FF)), _const_spec((D_FF, D_MODEL)),
                  _const_spec((1, D_MODEL))],
        out_specs=pl.BlockSpec((r, D_MODEL), tok),
        out_shape=jax.ShapeDtypeStruct((t, D_MODEL), F32),
        scratch_shapes=[pltpu.VMEM((r, D_FF), BF16)],
        compiler_params=pltpu.CompilerParams(
            dimension_semantics=("arbitrary",), vmem_limit_bytes=VMEM_LIMIT),
        name="conv_ffn",
    )(x, mix, h, h, h, w_up, conv_w, conv_b, w_down, norm_final)


def _rotary_tables(seq):
    half = HEAD_DIM // 2
    inv_freq = ROPE_BASE ** (-jnp.arange(half, dtype=F32) / half)
    ang = jnp.arange(seq, dtype=F32)[:, None] * inv_freq[None, :]
    cos = jnp.cos(ang)
    sin = jnp.sin(ang)
    return jnp.concatenate([cos, cos], axis=1), jnp.concatenate([-sin, sin], axis=1)


def _layer(x, prep):
    b, seq, _ = x.shape
    x2 = x.reshape(b * seq, D_MODEL)
    q, k, v, g, u = _in_proj(x2, prep["norm_mix"], prep["w_in"], prep["cos"], prep["sin"],
                             prep["gn_gain"], seq)
    ret = _retention(q, k, v, g, prep["ret_tabs"], seq)
    ssm = _s5(u, prep["s5"], seq)
    mix, h = _mix_norm(x2, ret, ssm, prep["glu_w"], prep["glu_b"], prep["w_out"], prep["norm_ffn"])
    out = _conv_ffn(x2, mix, h, prep["w_up"], prep["conv_w"], prep["conv_b"], prep["w_down"],
                    prep["norm_final"], seq)
    return out.reshape(b, seq, D_MODEL)


def kernel(x_prompt, x_sample, norm_mix, w_in, ret_gn_gain, s5_lambda_re, s5_lambda_im, s5_log_dt, s5_B_re, s5_B_im, s5_C_re, s5_C_im, s5_D, s5_glu_w, s5_glu_b, w_out, norm_ffn, w_up, conv_w, conv_b, w_down, norm_final):
    assert norm_mix.shape[0] == 1, "single-layer trunk"
    seq = x_prompt.shape[1]
    assert x_sample.shape[1] == seq
    cos, sin = _rotary_tables(seq)
    prep = {
        "norm_mix": norm_mix[0][None].astype(F32),
        "w_in": w_in[0].astype(BF16),
        "cos": cos, "sin": sin,
        "ret_tabs": _retention_tables(),
        "gn_gain": ret_gn_gain[0][None].astype(F32),
        "s5": _s5_tables(s5_lambda_re[0], s5_lambda_im[0], s5_log_dt[0], s5_B_re[0], s5_B_im[0],
                         s5_C_re[0], s5_C_im[0], s5_D[0]),
        "glu_w": s5_glu_w[0].astype(BF16),
        "glu_b": s5_glu_b[0][None].astype(F32),
        "w_out": w_out[0].astype(BF16),
        "norm_ffn": norm_ffn[0][None].astype(F32),
        "w_up": w_up[0].astype(BF16),
        "conv_w": conv_w[0].astype(F32),
        "conv_b": conv_b[0][None].astype(F32),
        "w_down": w_down[0].astype(BF16),
        "norm_final": norm_final[None].astype(F32),
    }
    return (_layer(x_prompt, prep), _layer(x_sample, prep))
```

```python
import functools

import jax
import jax.numpy as jnp
from jax import lax
from jax.experimental import pallas as pl
from jax.experimental.pallas import tpu as pltpu

D_MODEL = 1024
RET_WIDTH = 512
SSM_WIDTH = 512
RET_HEADS = 4
HEAD_DIM = 128
ROPE_BASE = 10000.0
SSM_GROUP = 16
SSM_GROUPS = 32
SSM_STATE = 64
D_FF = 2816
EPS = 1e-6
GELU_C0 = 0.7978845608028654
GELU_C1 = 0.044715
IN_WIDTH = 4 * RET_WIDTH + SSM_WIDTH

LANES = 128
SUBLANES = 8
BF16_ROWS = 16
VMEM_LIMIT = 56 * 1024 * 1024

RET_BLOCK = 256
SSM_BLOCKS = SSM_WIDTH // LANES
GROUPS_PER_BLOCK = LANES // SSM_GROUP
SSM_CHUNK = 16
SSM_SEQS = 4
TOK_TILE = 1024
FFN_TILE = 1024
MIX_TILE = 1024
FF_TILE = 256
HALO = BF16_ROWS

BF16 = jnp.bfloat16
F32 = jnp.float32


def _rms(x, w):
    return x * lax.rsqrt(jnp.mean(x * x, axis=-1, keepdims=True) + EPS) * w


def _const_spec(shape):
    nd = len(shape)
    return pl.BlockSpec(shape, lambda *_: (0,) * nd, pipeline_mode=pl.Buffered(1))


def _in_proj_kernel(x_ref, nw_ref, w_ref, cos_ref, sin_ref, gain_ref,
                    q_ref, k_ref, v_ref, g_ref, u_ref):
    h = _rms(x_ref[...], nw_ref[...]).astype(BF16)
    proj = jnp.dot(h, w_ref[...], preferred_element_type=F32)
    cos = cos_ref[...]
    sin = sin_ref[...]
    kscale = HEAD_DIM ** -0.5
    for hd in range(RET_HEADS):
        lo = hd * HEAD_DIM
        qh = proj[:, lo:lo + HEAD_DIM]
        q_ref[:, lo:lo + HEAD_DIM] = (qh * cos + pltpu.roll(qh, HEAD_DIM // 2, 1) * sin).astype(BF16)
        kh = proj[:, RET_WIDTH + lo:RET_WIDTH + lo + HEAD_DIM]
        kr = kh * cos + pltpu.roll(kh, HEAD_DIM // 2, 1) * sin
        k_ref[:, lo:lo + HEAD_DIM] = (kr * kscale).astype(BF16)
    v_ref[...] = proj[:, 2 * RET_WIDTH:3 * RET_WIDTH].astype(BF16)
    g_ref[...] = (jax.nn.silu(proj[:, 3 * RET_WIDTH:4 * RET_WIDTH]) * gain_ref[...]).astype(BF16)
    for blk in range(SSM_BLOCKS):
        lo = 4 * RET_WIDTH + blk * LANES
        u_ref[blk] = proj[:, lo:lo + LANES]


def _in_proj(x2, norm_w, w_in, cos2, sin2, gain, seq):
    t = x2.shape[0]
    r = TOK_TILE
    tiles_per_seq = seq // r
    tok = lambda i: (i, 0)
    out_tok = jax.ShapeDtypeStruct((t, RET_WIDTH), BF16)
    return pl.pallas_call(
        _in_proj_kernel,
        grid=(t // r,),
        in_specs=[
            pl.BlockSpec((r, D_MODEL), tok),
            _const_spec((1, D_MODEL)),
            _const_spec((D_MODEL, IN_WIDTH)),
            pl.BlockSpec((r, HEAD_DIM), lambda i: (i % tiles_per_seq, 0)),
            pl.BlockSpec((r, HEAD_DIM), lambda i: (i % tiles_per_seq, 0)),
            _const_spec((1, RET_WIDTH)),
        ],
        out_specs=[
            pl.BlockSpec((r, RET_WIDTH), tok),
            pl.BlockSpec((r, RET_WIDTH), tok),
            pl.BlockSpec((r, RET_WIDTH), tok),
            pl.BlockSpec((r, RET_WIDTH), tok),
            pl.BlockSpec((SSM_BLOCKS, r, LANES), lambda i: (0, i, 0)),
        ],
        out_shape=[out_tok, out_tok, out_tok, out_tok,
                   jax.ShapeDtypeStruct((SSM_BLOCKS, t, LANES), F32)],
        compiler_params=pltpu.CompilerParams(
            dimension_semantics=("arbitrary",), vmem_limit_bytes=VMEM_LIMIT),
        name="in_proj",
    )(x2, norm_w, w_in, cos2, sin2, gain)


def _retention_kernel(q_ref, k_ref, v_ref, g_ref, dmat_ref, cd_ref, row_ref, o_ref,
                      kv_scr, r_scr):
    c = RET_BLOCK
    nchunks = q_ref.shape[0] // c
    heads = [slice(hd * HEAD_DIM, (hd + 1) * HEAD_DIM) for hd in range(RET_HEADS)]

    def summarize(n, carry):
        rows = pl.ds(pl.multiple_of(n * c, c), c)
        for hd, cols in enumerate(heads):
            kn = k_ref[rows, cols]
            kcat = jnp.concatenate([kn * row_ref[hd, 0], kn * row_ref[hd, 1]], axis=1)
            kv_scr[n, hd] = lax.dot_general(kcat, v_ref[rows, cols], (((0,), (0,)), ((), ())),
                                            preferred_element_type=F32)
        return carry

    lax.fori_loop(0, nchunks, summarize, 0, unroll=2)

    for hd in range(RET_HEADS):
        cd = cd_ref[hd]
        rf = jnp.zeros((HEAD_DIM, HEAD_DIM), F32)
        for n in range(nchunks):
            r_scr[n, hd, 0:HEAD_DIM, :] = rf.astype(BF16)
            rf = cd * rf + kv_scr[n, hd, 0:HEAD_DIM, :]
        rb = jnp.zeros((HEAD_DIM, HEAD_DIM), F32)
        for n in reversed(range(nchunks)):
            r_scr[n, hd, HEAD_DIM:2 * HEAD_DIM, :] = rb.astype(BF16)
            rb = cd * rb + kv_scr[n, hd, HEAD_DIM:2 * HEAD_DIM, :]

    def emit(n, carry):
        rows = pl.ds(pl.multiple_of(n * c, c), c)
        for hd, cols in enumerate(heads):
            qn = q_ref[rows, cols]
            s = lax.dot_general(qn, k_ref[rows, cols], (((1,), (1,)), ((), ())),
                                preferred_element_type=F32) * dmat_ref[hd]
            inner = jnp.dot(s.astype(BF16), v_ref[rows, cols], preferred_element_type=F32)
            qcat = jnp.concatenate([qn * row_ref[hd, 2], qn * row_ref[hd, 3]], axis=1)
            o = inner + jnp.dot(qcat, r_scr[n, hd], preferred_element_type=F32)
            o = o * lax.rsqrt(jnp.mean(o * o, axis=-1, keepdims=True) + EPS)
            o_ref[rows, cols] = (g_ref[rows, cols].astype(F32) * o).astype(BF16)
        return carry

    lax.fori_loop(0, nchunks, emit, 0, unroll=4)


def _retention(q, k, v, g, tabs, seq):
    dmat, cd, tab_rows = tabs
    t = q.shape[0]
    nchunks = seq // RET_BLOCK
    blk = pl.BlockSpec((seq, RET_WIDTH), lambda b: (b, 0))
    return pl.pallas_call(
        _retention_kernel,
        grid=(t // seq,),
        in_specs=[blk, blk, blk, blk,
                  _const_spec((RET_HEADS, RET_BLOCK, RET_BLOCK)),
                  _const_spec((RET_HEADS, HEAD_DIM, HEAD_DIM)),
                  _const_spec((RET_HEADS, 4, RET_BLOCK, HEAD_DIM))],
        out_specs=blk,
        out_shape=jax.ShapeDtypeStruct((t, RET_WIDTH), BF16),
        scratch_shapes=[pltpu.VMEM((nchunks, RET_HEADS, 2 * HEAD_DIM, HEAD_DIM), F32),
                        pltpu.VMEM((nchunks, RET_HEADS, 2 * HEAD_DIM, HEAD_DIM), BF16)],
        compiler_params=pltpu.CompilerParams(
            dimension_semantics=("arbitrary",), vmem_limit_bytes=VMEM_LIMIT),
        name="retention",
    )(q, k, v, g, dmat, cd, tab_rows)


def _retention_tables():
    c = RET_BLOCK
    lg = jnp.log(1.0 - 2.0 ** (-5.0 - jnp.arange(RET_HEADS, dtype=F32)))[:, None, None]
    idx = jnp.arange(c, dtype=F32)
    col = jnp.broadcast_to(idx[:, None], (c, HEAD_DIM))[None]
    dmat = jnp.exp(lg * jnp.abs(idx[:, None] - idx[None, :])[None])
    cd = jnp.exp(lg * c) * jnp.ones((1, HEAD_DIM, HEAD_DIM), F32)
    zf = jnp.exp(lg * (c - 1.0 - col))
    zb = jnp.exp(lg * col)
    xf = jnp.exp(lg * (col + 1.0))
    xb = jnp.exp(lg * (c - col))
    return dmat, cd, jnp.stack([zf, zb, xf, xb], axis=1).astype(BF16)


def _cmul(ar, ai, br, bi):
    return ar * br - ai * bi, ar * bi + ai * br


def _roll_lanes(x, shift):
    ntiles = x.shape[1] // LANES
    whole, part = divmod(shift % x.shape[1], LANES)
    tiles = [x[:, k * LANES:(k + 1) * LANES] for k in range(ntiles)]
    tiles = [tiles[(k - whole) % ntiles] for k in range(ntiles)]
    if part:
        lane = lax.broadcasted_iota(jnp.int32, tiles[0].shape, 1)
        rolled = [pltpu.roll(t, part, 1) for t in tiles]
        tiles = [jnp.where(lane < part, rolled[(k - 1) % ntiles], rolled[k]) for k in range(ntiles)]
    return tiles[0] if ntiles == 1 else jnp.concatenate(tiles, axis=1)


def _split_dot(a, z):
    dims = (((1,), (1,)), ((), ()))
    a_hi = a.astype(BF16)
    z_hi = z.astype(BF16)
    a_lo = (a - a_hi.astype(F32)).astype(BF16)
    z_lo = (z - z_hi.astype(F32)).astype(BF16)
    return (lax.dot_general(a_hi, z_hi, dims, preferred_element_type=F32)
            + lax.dot_general(a_hi, z_lo, dims, preferred_element_type=F32)
            + lax.dot_general(a_lo, z_hi, dims, preferred_element_type=F32))


def _s5_kernel(u_ref, perm_ref, bt_ref, ct_ref, pw_ref, dt0_ref, tab_ref, y_ref,
               m_scr, w_scr, vt_scr, g_scr, u_scr, s_scr, x_scr, *, nseq, seq):
    c = SSM_CHUNK
    sub = c // SUBLANES
    gpb = GROUPS_PER_BLOCK
    npair = gpb // 2
    width = c * SSM_GROUP
    sp = npair * LANES
    rows8 = nseq * seq // SUBLANES
    rows = rows8 // sub
    nrows = seq // c

    @pl.when(pl.program_id(1) == 0)
    def _expand():
        slot = lax.broadcasted_iota(jnp.int32, (SSM_GROUP, width), 1) >> 4
        for g in range(gpb):
            pair, gi = divmod(g, 2)
            zt = ([], [])
            for d in range(2):
                b_re, b_im = bt_ref[d, g, 0], bt_ref[d, g, 1]
                c_re, c_im = ct_ref[d, g, 0], ct_ref[d, g, 1]
                for s in range(c):
                    r16 = slice(gi * width + s * SSM_GROUP, gi * width + (s + 1) * SSM_GROUP)
                    re_cols = slice((2 * d) * LANES, (2 * d + 1) * LANES)
                    im_cols = slice((2 * d + 1) * LANES, (2 * d + 2) * LANES)
                    p_w = c - 1 - s if d == 0 else s
                    p_v = s + 1 if d == 0 else c - s
                    p_z = s if d == 0 else c - 1 - s
                    wr, wi = _cmul(b_re, b_im, pw_ref[p_w, d, pair, 0], pw_ref[p_w, d, pair, 1])
                    w_scr[pair, r16, re_cols] = wr.astype(BF16)
                    w_scr[pair, r16, im_cols] = wi.astype(BF16)
                    er, ei = _cmul(c_re, c_im, pw_ref[p_v, d, pair, 0], pw_ref[p_v, d, pair, 1])
                    vt_scr[pair, r16, re_cols] = er.astype(BF16)
                    vt_scr[pair, r16, im_cols] = (-ei).astype(BF16)
                    er, ei = _cmul(c_re, c_im, pw_ref[p_z, d, pair, 0], pw_ref[p_z, d, pair, 1])
                    zt[d].append(jnp.concatenate([er, -ei], axis=1))
            bcat = [jnp.concatenate([bt_ref[d, g, 0], bt_ref[d, g, 1]], axis=1) for d in range(2)]
            f0 = _split_dot(bcat[0], jnp.concatenate(zt[0], axis=0)) + dt0_ref[g]
            b0 = _split_dot(bcat[1], jnp.concatenate(zt[1], axis=0))
            for s in range(c):
                fwd = jnp.where(slot >= s, _roll_lanes(f0, SSM_GROUP * s), 0.0)
                bwd = jnp.where(slot <= s, _roll_lanes(b0, SSM_GROUP * (s + 1)), 0.0)
                m_scr[g, s * SSM_GROUP:(s + 1) * SSM_GROUP, :] = (fwd + bwd).astype(BF16)

    ucat = jnp.concatenate(
        [u_ref[pl.ds(t, rows8, stride=SUBLANES), :].astype(BF16) for t in range(SUBLANES)], axis=1)
    grouped = jnp.dot(ucat, perm_ref[...], preferred_element_type=F32)
    for g in range(gpb):
        g_scr[g] = grouped[:, g * LANES:(g + 1) * LANES]
    for g in range(gpb):
        u_scr[g] = jnp.concatenate(
            [g_scr[g, pl.ds(k, rows, stride=sub), :] for k in range(sub)], axis=1).astype(BF16)

    for pair in range(npair):
        upair = jnp.concatenate([u_scr[2 * pair], u_scr[2 * pair + 1]], axis=1)
        sg = jnp.dot(upair, w_scr[pair], preferred_element_type=F32)
        for k in range(4):
            s_scr[:, k * sp + pair * LANES:k * sp + (pair + 1) * LANES] = sg[:, k * LANES:(k + 1) * LANES]

    nblk = nrows // SUBLANES
    row = lax.broadcasted_iota(jnp.int32, (SUBLANES, sp), 0)

    def scan_block(base, d, blk, carry):
        rows_ = pl.ds(pl.multiple_of(base + blk * SUBLANES, SUBLANES), SUBLANES)
        lo = d * 2 * sp
        pr = s_scr[rows_, lo:lo + sp]
        pi = s_scr[rows_, lo + sp:lo + 2 * sp]
        for step, sh in enumerate((1, 2, 4)):
            shift = sh if d == 0 else SUBLANES - sh
            tr, ti = _cmul(tab_ref[d, step, 0], tab_ref[d, step, 1],
                           pltpu.roll(pr, shift, 0), pltpu.roll(pi, shift, 0))
            pr = pr + tr
            pi = pi + ti
        cr, ci = carry
        tr, ti = _cmul(tab_ref[d, 3, 0], tab_ref[d, 3, 1], cr, ci)
        xr = pr + tr
        xi = pi + ti
        edge = 0 if d == 0 else SUBLANES - 1
        shift = 1 if d == 0 else SUBLANES - 1
        x_scr[rows_, lo:lo + sp] = jnp.where(row == edge, cr, pltpu.roll(xr, shift, 0))
        x_scr[rows_, lo + sp:lo + 2 * sp] = jnp.where(row == edge, ci, pltpu.roll(xi, shift, 0))
        last = SUBLANES - 1 if d == 0 else 0
        return (jnp.broadcast_to(xr[last:last + 1, :], (SUBLANES, sp)),
                jnp.broadcast_to(xi[last:last + 1, :], (SUBLANES, sp)))

    zero = jnp.zeros((SUBLANES, sp), F32)

    def body(i, carry):
        out = ()
        for sq in range(nseq):
            cf = scan_block(sq * nrows, 0, i, carry[4 * sq:4 * sq + 2])
            cb = scan_block(sq * nrows, 1, nblk - 1 - i, carry[4 * sq + 2:4 * sq + 4])
            out = out + cf + cb
        return out

    lax.fori_loop(0, nblk, body, (zero,) * (4 * nseq))

    for pair in range(npair):
        xpair = jnp.concatenate(
            [x_scr[:, k * sp + pair * LANES:k * sp + (pair + 1) * LANES] for k in range(4)],
            axis=1).astype(BF16)
        cross = lax.dot_general(xpair, vt_scr[pair], (((1,), (1,)), ((), ())),
                                preferred_element_type=F32)
        for gi in range(2):
            g = 2 * pair + gi
            yg = (jnp.dot(u_scr[g], m_scr[g], preferred_element_type=F32)
                  + cross[:, gi * width:(gi + 1) * width])
            for k in range(sub):
                g_scr[g, pl.ds(k, rows, stride=sub), :] = yg[:, k * LANES:(k + 1) * LANES]

    ycat = jnp.concatenate([g_scr[g] for g in range(gpb)], axis=1).astype(BF16)
    ytok = jnp.dot(ycat, perm_ref[...], preferred_element_type=F32)
    for t in range(SUBLANES):
        y_ref[pl.ds(t, rows8, stride=SUBLANES), :] = ytok[:, t * LANES:(t + 1) * LANES]


def _s5(u_blocks, mats, seq):
    perm, bt, ct, pw, dt0, tabs = mats
    nb, t, _ = u_blocks.shape
    c = SSM_CHUNK
    gpb = GROUPS_PER_BLOCK
    npair = gpb // 2
    nseq = SSM_SEQS
    while (t // seq) % nseq:
        nseq //= 2
    rows8 = nseq * seq // SUBLANES
    rows = nseq * seq // c
    width = c * SSM_GROUP
    sp = npair * LANES
    return pl.pallas_call(
        functools.partial(_s5_kernel, nseq=nseq, seq=seq),
        grid=(nb, t // (nseq * seq)),
        in_specs=[
            pl.BlockSpec((None, nseq * seq, LANES), lambda b, i: (b, i, 0)),
            _const_spec((SUBLANES * LANES, SUBLANES * LANES)),
            pl.BlockSpec((None, 2, gpb, 2, SSM_GROUP, LANES), lambda b, i: (b, 0, 0, 0, 0, 0)),
            pl.BlockSpec((None, 2, gpb, 2, SSM_GROUP, LANES), lambda b, i: (b, 0, 0, 0, 0, 0)),
            pl.BlockSpec((None, c + 1, 2, npair, 2, 1, LANES),
                         lambda b, i: (b, 0, 0, 0, 0, 0, 0)),
            pl.BlockSpec((None, gpb, SSM_GROUP, width), lambda b, i: (b, 0, 0, 0)),
            pl.BlockSpec((None, 2, 4, 2, SUBLANES, sp), lambda b, i: (b, 0, 0, 0, 0, 0)),
        ],
        out_specs=pl.BlockSpec((None, nseq * seq, LANES), lambda b, i: (b, i, 0)),
        out_shape=jax.ShapeDtypeStruct((nb, t, LANES), F32),
        scratch_shapes=[pltpu.VMEM((gpb, width, width), BF16),
                        pltpu.VMEM((npair, 2 * width, 4 * LANES), BF16),
                        pltpu.VMEM((npair, 2 * width, 4 * LANES), BF16),
                        pltpu.VMEM((gpb, rows8, LANES), F32),
                        pltpu.VMEM((gpb, rows, width), BF16),
                        pltpu.VMEM((rows, 4 * sp), F32),
                        pltpu.VMEM((rows, 4 * sp), F32)],
        compiler_params=pltpu.CompilerParams(
            dimension_semantics=("arbitrary", "arbitrary"), vmem_limit_bytes=VMEM_LIMIT),
        name="s5",
    )(u_blocks, perm, bt, ct, pw, dt0, tabs)


def _s5_tables(lam_re, lam_im, log_dt, b_re, b_im, c_re, c_im, d_skip):
    c = SSM_CHUNK
    nb, gpb = SSM_BLOCKS, GROUPS_PER_BLOCK
    npair = gpb // 2
    lr = jnp.minimum(lam_re.astype(F32), -1e-4)
    li = lam_im.astype(F32)
    dt = jnp.exp(log_dt.astype(F32))[..., None]
    plist = list(range(c + 1)) + [c * k for k in range(2, SUBLANES + 1)]
    pidx = {pw_: i for i, pw_ in enumerate(plist)}
    p = jnp.asarray(plist, F32)[:, None, None, None]
    pmag = jnp.exp(p * (lr * dt)[None])
    pr = pmag * jnp.cos(p * (li * dt)[None])
    pi = pmag * jnp.sin(p * (li * dt)[None])
    ar, ai = pr[1], pi[1]
    den = lr * lr + li * li
    nr = ar - 1.0
    ni = ai
    cr = ((nr * lr + ni * li) / den)[:, :, None, :]
    ci = ((ni * lr - nr * li) / den)[:, :, None, :]
    b_re_t = b_re.astype(F32).transpose(0, 1, 3, 2)
    b_im_t = b_im.astype(F32).transpose(0, 1, 3, 2)
    bbar = jnp.stack([cr * b_re_t - ci * b_im_t, cr * b_im_t + ci * b_re_t], axis=2)
    cmat = jnp.stack([c_re.astype(F32), c_im.astype(F32)], axis=2)

    def half_placed(x):
        own = (jnp.arange(SSM_GROUPS) % 2)[:, None] == jnp.arange(2)[None, :]
        wide = jnp.where(own[None, :, None, None, :, None], x[:, :, :, :, None, :], 0.0)
        wide = wide.reshape(2, nb, gpb, 2, SSM_GROUP, LANES)
        return jnp.moveaxis(wide, 1, 0)

    pows = jnp.stack([pr, pi], axis=3)[:c + 1]
    pw = pows.reshape(c + 1, 2, nb, npair, 2, 2, SSM_STATE).transpose(2, 0, 1, 3, 5, 4, 6)
    pw = pw.reshape(nb, c + 1, 2, npair, 2, 1, LANES)

    lane = jnp.arange(c * SSM_GROUP)
    dt0 = jnp.where(lane[None, None, :] == jnp.arange(SSM_GROUP)[None, :, None],
                    d_skip.astype(F32).reshape(SSM_GROUPS, SSM_GROUP, 1), 0.0)
    dt0 = dt0.reshape(nb, gpb, SSM_GROUP, c * SSM_GROUP)

    rowi = jnp.arange(SUBLANES)

    def lanes(x):
        return x.reshape(x.shape[:-2] + (nb, npair * LANES))

    tabs = []
    for d in range(2):
        per_dir = []
        for sh in (1, 2, 4):
            keep = (rowi >= sh) if d == 0 else (rowi < SUBLANES - sh)
            per_dir.append(jnp.stack([
                jnp.where(keep[:, None, None], lanes(pr[pidx[sh * c], d])[None], 0.0),
                jnp.where(keep[:, None, None], lanes(pi[pidx[sh * c], d])[None], 0.0)]))
        mult = range(1, SUBLANES + 1) if d == 0 else range(SUBLANES, 0, -1)
        cpow = jnp.asarray([pidx[k * c] for k in mult])
        per_dir.append(jnp.stack([lanes(pr[cpow, d]), lanes(pi[cpow, d])]))
        tabs.append(jnp.stack(per_dir))
    tabs = jnp.stack(tabs).transpose(4, 0, 1, 2, 3, 5)

    src = jnp.arange(SUBLANES * LANES)
    dst = ((src >> 4) & 7) * LANES + (src >> 7) * SSM_GROUP + (src & 15)
    perm = (dst[:, None] == src[None, :]).astype(BF16)
    return perm, half_placed(bbar), half_placed(cmat), pw, dt0, tabs


def _mix_norm_kernel(x_ref, ret_ref, ssm_ref, glu_w_ref, glu_b_ref, w_out_ref, nffn_ref,
                     mix_ref, h_ref):
    ssm = jnp.concatenate([ssm_ref[b] for b in range(SSM_BLOCKS)], axis=1)
    y = jax.nn.gelu(ssm)
    gate = jax.nn.sigmoid(
        jnp.dot(y.astype(BF16), glu_w_ref[...], preferred_element_type=F32) + glu_b_ref[...])
    cat = jnp.concatenate([ret_ref[...], (y * gate).astype(BF16)], axis=1)
    mix = jnp.dot(cat, w_out_ref[...], preferred_element_type=F32).astype(BF16)
    mix_ref[...] = mix
    h_ref[...] = _rms(x_ref[...] + mix.astype(F32), nffn_ref[...]).astype(BF16)


def _mix_norm(x2, ret, ssm, glu_w, glu_b, w_out, norm_ffn):
    t = x2.shape[0]
    r = MIX_TILE
    tok = lambda i: (i, 0)
    return pl.pallas_call(
        _mix_norm_kernel,
        grid=(t // r,),
        in_specs=[pl.BlockSpec((r, D_MODEL), tok), pl.BlockSpec((r, RET_WIDTH), tok),
                  pl.BlockSpec((SSM_BLOCKS, r, LANES), lambda i: (0, i, 0)),
                  _const_spec((SSM_WIDTH, SSM_WIDTH)), _const_spec((1, SSM_WIDTH)),
                  _const_spec((D_MODEL, D_MODEL)), _const_spec((1, D_MODEL))],
        out_specs=[pl.BlockSpec((r, D_MODEL), tok), pl.BlockSpec((r, D_MODEL), tok)],
        out_shape=[jax.ShapeDtypeStruct((t, D_MODEL), BF16),
                   jax.ShapeDtypeStruct((t, D_MODEL), BF16)],
        compiler_params=pltpu.CompilerParams(
            dimension_semantics=("arbitrary",), vmem_limit_bytes=VMEM_LIMIT),
        name="mix_norm",
    )(x2, ret, ssm, glu_w, glu_b, w_out, norm_ffn)


def _conv_ffn_kernel(x_ref, mix_ref, hm_ref, hp_ref, hn_ref, w_up_ref, cw_ref, cb_ref,
                     w_down_ref, nfin_ref, o_ref, hid_scr, *, tiles_per_seq):
    r = hm_ref.shape[0]
    ext = r + 2 * HALO
    j = pl.program_id(0) % tiles_per_seq
    hp = jnp.where(j == 0, jnp.zeros(hp_ref.shape, BF16), hp_ref[...])
    hn = jnp.where(j == tiles_per_seq - 1, jnp.zeros(hn_ref.shape, BF16), hn_ref[...])
    h = jnp.concatenate([hp, hm_ref[...], hn], axis=0)
    for jj in range(D_FF // FF_TILE):
        parts = []
        for off in (jj * FF_TILE, D_FF + jj * FF_TILE):
            z = jnp.dot(h, w_up_ref[:, off:off + FF_TILE], preferred_element_type=F32)
            zc = (pltpu.roll(z, 1, 0)[HALO:HALO + r] * cw_ref[0:1, off:off + FF_TILE]
                  + z[HALO:HALO + r] * cw_ref[1:2, off:off + FF_TILE]
                  + pltpu.roll(z, ext - 1, 0)[HALO:HALO + r] * cw_ref[2:3, off:off + FF_TILE]
                  + cb_ref[:, off:off + FF_TILE])
            parts.append(zc)
        val, gate = parts
        t = jnp.tanh(gate * (GELU_C0 + (GELU_C0 * GELU_C1) * (gate * gate)))
        hid_scr[:, jj * FF_TILE:(jj + 1) * FF_TILE] = ((0.5 * gate * val) * (1.0 + t)).astype(BF16)
    x1 = x_ref[...] + mix_ref[...].astype(F32)
    x2 = x1 + jnp.dot(hid_scr[...], w_down_ref[...], preferred_element_type=F32)
    o_ref[...] = _rms(x2, nfin_ref[...])


def _conv_ffn(x, mix, h, w_up, conv_w, conv_b, w_down, norm_final, seq):
    t = x.shape[0]
    r = FFN_TILE
    tiles_per_seq = seq // r
    hb = r // HALO
    last = t // HALO - 1
    tok = lambda i: (i, 0)
    return pl.pallas_call(
        functools.partial(_conv_ffn_kernel, tiles_per_seq=tiles_per_seq),
        grid=(t // r,),
        in_specs=[pl.BlockSpec((r, D_MODEL), tok), pl.BlockSpec((r, D_MODEL), tok),
                  pl.BlockSpec((r, D_MODEL), tok),
                  pl.BlockSpec((HALO, D_MODEL), lambda i: (jnp.maximum(i * hb - 1, 0), 0)),
                  pl.BlockSpec((HALO, D_MODEL), lambda i: (jnp.minimum((i + 1) * hb, last), 0)),
                  _const_spec((D_MODEL, 2 * D_FF)), _const_spec((3, 2 * D_FF)),
                  _const_spec((1, 2 * D_FF)), _const_spec((D_FF, D_MODEL)),
                  _const_spec((1, D_MODEL))],
        out_specs=pl.BlockSpec((r, D_MODEL), tok),
        out_shape=jax.ShapeDtypeStruct((t, D_MODEL), F32),
        scratch_shapes=[pltpu.VMEM((r, D_FF), BF16)],
        compiler_params=pltpu.CompilerParams(
            dimension_semantics=("arbitrary",), vmem_limit_bytes=VMEM_LIMIT),
        name="conv_ffn",
    )(x, mix, h, h, h, w_up, conv_w, conv_b, w_down, norm_final)


def _rotary_tables(seq):
    half = HEAD_DIM // 2
    inv_freq = ROPE_BASE ** (-jnp.arange(half, dtype=F32) / half)
    ang = jnp.arange(seq, dtype=F32)[:, None] * inv_freq[None, :]
    cos = jnp.cos(ang)
    sin = jnp.sin(ang)
    return jnp.concatenate([cos, cos], axis=1), jnp.concatenate([-sin, sin], axis=1)


def _layer(x, prep):
    b, seq, _ = x.shape
    x2 = x.reshape(b * seq, D_MODEL)
    q, k, v, g, u = _in_proj(x2, prep["norm_mix"], prep["w_in"], prep["cos"], prep["sin"],
                             prep["gn_gain"], seq)
    ret = _retention(q, k, v, g, prep["ret_tabs"], seq)
    ssm = _s5(u, prep["s5"], seq)
    mix, h = _mix_norm(x2, ret, ssm, prep["glu_w"], prep["glu_b"], prep["w_out"], prep["norm_ffn"])
    out = _conv_ffn(x2, mix, h, prep["w_up"], prep["conv_w"], prep["conv_b"], prep["w_down"],
                    prep["norm_final"], seq)
    return out.reshape(b, seq, D_MODEL)


def kernel(x_prompt, x_sample, norm_mix, w_in, ret_gn_gain, s5_lambda_re, s5_lambda_im, s5_log_dt, s5_B_re, s5_B_im, s5_C_re, s5_C_im, s5_D, s5_glu_w, s5_glu_b, w_out, norm_ffn, w_up, conv_w, conv_b, w_down, norm_final):
    assert norm_mix.shape[0] == 1, "single-layer trunk"
    seq = x_prompt.shape[1]
    assert x_sample.shape[1] == seq
    cos, sin = _rotary_tables(seq)
    prep = {
        "norm_mix": norm_mix[0][None].astype(F32),
        "w_in": w_in[0].astype(BF16),
        "cos": cos, "sin": sin,
        "ret_tabs": _retention_tables(),
        "gn_gain": ret_gn_gain[0][None].astype(F32),
        "s5": _s5_tables(s5_lambda_re[0], s5_lambda_im[0], s5_log_dt[0], s5_B_re[0], s5_B_im[0],
                         s5_C_re[0], s5_C_im[0], s5_D[0]),
        "glu_w": s5_glu_w[0].astype(BF16),
        "glu_b": s5_glu_b[0][None].astype(F32),
        "w_out": w_out[0].astype(BF16),
        "norm_ffn": norm_ffn[0][None].astype(F32),
        "w_up": w_up[0].astype(BF16),
        "conv_w": conv_w[0].astype(F32),
        "conv_b": conv_b[0][None].astype(F32),
        "w_down": w_down[0].astype(BF16),
        "norm_final": norm_final[None].astype(F32),
    }
    return (_layer(x_prompt, prep), _layer(x_sample, prep))
```

```python
import functools

import jax
import jax.numpy as jnp
from jax import lax
from jax.experimental import pallas as pl
from jax.experimental.pallas import tpu as pltpu

D_MODEL = 1024
RET_WIDTH = 512
SSM_WIDTH = 512
RET_HEADS = 4
HEAD_DIM = 128
ROPE_BASE = 10000.0
SSM_GROUP = 16
SSM_GROUPS = 32
SSM_STATE = 64
D_FF = 2816
EPS = 1e-6
GELU_C0 = 0.7978845608028654
GELU_C1 = 0.044715
IN_WIDTH = 4 * RET_WIDTH + SSM_WIDTH

LANES = 128
SUBLANES = 8
BF16_ROWS = 16
VMEM_LIMIT = 56 * 1024 * 1024

RET_BLOCK = 256
SSM_BLOCKS = SSM_WIDTH // LANES
GROUPS_PER_BLOCK = LANES // SSM_GROUP
SSM_CHUNK = 16
SSM_SEQS = 4
TOK_TILE = 1024
FFN_TILE = 1024
MIX_TILE = 1024
FF_TILE = 256
HALO = BF16_ROWS

BF16 = jnp.bfloat16
F32 = jnp.float32


def _rms(x, w):
    return x * lax.rsqrt(jnp.mean(x * x, axis=-1, keepdims=True) + EPS) * w


def _const_spec(shape):
    nd = len(shape)
    return pl.BlockSpec(shape, lambda *_: (0,) * nd, pipeline_mode=pl.Buffered(1))


def _in_proj_kernel(x_ref, nw_ref, w_ref, cos_ref, sin_ref, gain_ref,
                    q_ref, k_ref, v_ref, g_ref, u_ref):
    h = _rms(x_ref[...], nw_ref[...]).astype(BF16)
    proj = jnp.dot(h, w_ref[...], preferred_element_type=F32)
    cos = cos_ref[...]
    sin = sin_ref[...]
    kscale = HEAD_DIM ** -0.5
    for hd in range(RET_HEADS):
        lo = hd * HEAD_DIM
        qh = proj[:, lo:lo + HEAD_DIM]
        q_ref[:, lo:lo + HEAD_DIM] = (qh * cos + pltpu.roll(qh, HEAD_DIM // 2, 1) * sin).astype(BF16)
        kh = proj[:, RET_WIDTH + lo:RET_WIDTH + lo + HEAD_DIM]
        kr = kh * cos + pltpu.roll(kh, HEAD_DIM // 2, 1) * sin
        k_ref[:, lo:lo + HEAD_DIM] = (kr * kscale).astype(BF16)
    v_ref[...] = proj[:, 2 * RET_WIDTH:3 * RET_WIDTH].astype(BF16)
    g_ref[...] = (jax.nn.silu(proj[:, 3 * RET_WIDTH:4 * RET_WIDTH]) * gain_ref[...]).astype(BF16)
    for blk in range(SSM_BLOCKS):
        lo = 4 * RET_WIDTH + blk * LANES
        u_ref[blk] = proj[:, lo:lo + LANES]


def _in_proj(x2, norm_w, w_in, cos2, sin2, gain, seq):
    t = x2.shape[0]
    r = TOK_TILE
    assert seq % r == 0, "token tiles must not straddle sequences"
    tiles_per_seq = seq // r
    tok = lambda i: (i, 0)
    out_tok = jax.ShapeDtypeStruct((t, RET_WIDTH), BF16)
    return pl.pallas_call(
        _in_proj_kernel,
        grid=(t // r,),
        in_specs=[
            pl.BlockSpec((r, D_MODEL), tok),
            _const_spec((1, D_MODEL)),
            _const_spec((D_MODEL, IN_WIDTH)),
            pl.BlockSpec((r, HEAD_DIM), lambda i: (i % tiles_per_seq, 0)),
            pl.BlockSpec((r, HEAD_DIM), lambda i: (i % tiles_per_seq, 0)),
            _const_spec((1, RET_WIDTH)),
        ],
        out_specs=[
            pl.BlockSpec((r, RET_WIDTH), tok),
            pl.BlockSpec((r, RET_WIDTH), tok),
            pl.BlockSpec((r, RET_WIDTH), tok),
            pl.BlockSpec((r, RET_WIDTH), tok),
            pl.BlockSpec((SSM_BLOCKS, r, LANES), lambda i: (0, i, 0)),
        ],
        out_shape=[out_tok, out_tok, out_tok, out_tok,
                   jax.ShapeDtypeStruct((SSM_BLOCKS, t, LANES), F32)],
        compiler_params=pltpu.CompilerParams(
            dimension_semantics=("arbitrary",), vmem_limit_bytes=VMEM_LIMIT),
        name="in_proj",
    )(x2, norm_w, w_in, cos2, sin2, gain)


def _retention_kernel(q_ref, k_ref, v_ref, g_ref, dmat_ref, cd_ref, row_ref, o_ref,
                      kv_scr, r_scr):
    c = RET_BLOCK
    nchunks = q_ref.shape[0] // c
    heads = [slice(hd * HEAD_DIM, (hd + 1) * HEAD_DIM) for hd in range(RET_HEADS)]

    def summarize(n, carry):
        rows = pl.ds(pl.multiple_of(n * c, c), c)
        for hd, cols in enumerate(heads):
            kn = k_ref[rows, cols]
            kcat = jnp.concatenate([kn * row_ref[hd, 0], kn * row_ref[hd, 1]], axis=1)
            kv_scr[n, hd] = lax.dot_general(kcat, v_ref[rows, cols], (((0,), (0,)), ((), ())),
                                            preferred_element_type=F32)
        return carry

    lax.fori_loop(0, nchunks, summarize, 0, unroll=2)

    for hd in range(RET_HEADS):
        cd = cd_ref[hd]
        rf = jnp.zeros((HEAD_DIM, HEAD_DIM), F32)
        for n in range(nchunks):
            r_scr[n, hd, 0:HEAD_DIM, :] = rf.astype(BF16)
            rf = cd * rf + kv_scr[n, hd, 0:HEAD_DIM, :]
        rb = jnp.zeros((HEAD_DIM, HEAD_DIM), F32)
        for n in reversed(range(nchunks)):
            r_scr[n, hd, HEAD_DIM:2 * HEAD_DIM, :] = rb.astype(BF16)
            rb = cd * rb + kv_scr[n, hd, HEAD_DIM:2 * HEAD_DIM, :]

    def emit(n, carry):
        rows = pl.ds(pl.multiple_of(n * c, c), c)
        for hd, cols in enumerate(heads):
            qn = q_ref[rows, cols]
            s = lax.dot_general(qn, k_ref[rows, cols], (((1,), (1,)), ((), ())),
                                preferred_element_type=F32) * dmat_ref[hd]
            inner = jnp.dot(s.astype(BF16), v_ref[rows, cols], preferred_element_type=F32)
            qcat = jnp.concatenate([qn * row_ref[hd, 2], qn * row_ref[hd, 3]], axis=1)
            o = inner + jnp.dot(qcat, r_scr[n, hd], preferred_element_type=F32)
            o = o * lax.rsqrt(jnp.mean(o * o, axis=-1, keepdims=True) + EPS)
            o_ref[rows, cols] = (g_ref[rows, cols].astype(F32) * o).astype(BF16)
        return carry

    lax.fori_loop(0, nchunks, emit, 0, unroll=4)


def _retention(q, k, v, g, tabs, seq):
    dmat, cd, tab_rows = tabs
    t = q.shape[0]
    nchunks = seq // RET_BLOCK
    blk = pl.BlockSpec((seq, RET_WIDTH), lambda b: (b, 0))
    return pl.pallas_call(
        _retention_kernel,
        grid=(t // seq,),
        in_specs=[blk, blk, blk, blk,
                  _const_spec((RET_HEADS, RET_BLOCK, RET_BLOCK)),
                  _const_spec((RET_HEADS, HEAD_DIM, HEAD_DIM)),
                  _const_spec((RET_HEADS, 4, RET_BLOCK, HEAD_DIM))],
        out_specs=blk,
        out_shape=jax.ShapeDtypeStruct((t, RET_WIDTH), BF16),
        scratch_shapes=[pltpu.VMEM((nchunks, RET_HEADS, 2 * HEAD_DIM, HEAD_DIM), F32),
                        pltpu.VMEM((nchunks, RET_HEADS, 2 * HEAD_DIM, HEAD_DIM), BF16)],
        compiler_params=pltpu.CompilerParams(
            dimension_semantics=("arbitrary",), vmem_limit_bytes=VMEM_LIMIT),
        name="retention",
    )(q, k, v, g, dmat, cd, tab_rows)


def _retention_tables():
    c = RET_BLOCK
    lg = jnp.log(1.0 - 2.0 ** (-5.0 - jnp.arange(RET_HEADS, dtype=F32)))[:, None, None]
    idx = jnp.arange(c, dtype=F32)
    col = jnp.broadcast_to(idx[:, None], (c, HEAD_DIM))[None]
    dmat = jnp.exp(lg * jnp.abs(idx[:, None] - idx[None, :])[None])
    cd = jnp.exp(lg * c) * jnp.ones((1, HEAD_DIM, HEAD_DIM), F32)
    zf = jnp.exp(lg * (c - 1.0 - col))
    zb = jnp.exp(lg * col)
    xf = jnp.exp(lg * (col + 1.0))
    xb = jnp.exp(lg * (c - col))
    return dmat, cd, jnp.stack([zf, zb, xf, xb], axis=1).astype(BF16)


def _cmul(ar, ai, br, bi):
    return ar * br - ai * bi, ar * bi + ai * br


def _roll_lanes(x, shift):
    ntiles = x.shape[1] // LANES
    whole, part = divmod(shift % x.shape[1], LANES)
    tiles = [x[:, k * LANES:(k + 1) * LANES] for k in range(ntiles)]
    tiles = [tiles[(k - whole) % ntiles] for k in range(ntiles)]
    if part:
        lane = lax.broadcasted_iota(jnp.int32, tiles[0].shape, 1)
        rolled = [pltpu.roll(t, part, 1) for t in tiles]
        tiles = [jnp.where(lane < part, rolled[(k - 1) % ntiles], rolled[k]) for k in range(ntiles)]
    return tiles[0] if ntiles == 1 else jnp.concatenate(tiles, axis=1)


def _split_dot(a, z):
    dims = (((1,), (1,)), ((), ()))
    a_hi = a.astype(BF16)
    z_hi = z.astype(BF16)
    a_lo = (a - a_hi.astype(F32)).astype(BF16)
    z_lo = (z - z_hi.astype(F32)).astype(BF16)
    return (lax.dot_general(a_hi, z_hi, dims, preferred_element_type=F32)
            + lax.dot_general(a_hi, z_lo, dims, preferred_element_type=F32)
            + lax.dot_general(a_lo, z_hi, dims, preferred_element_type=F32))


def _s5_kernel(u_ref, perm_ref, bt_ref, ct_ref, pw_ref, dt0_ref, tab_ref, y_ref,
               m_scr, w_scr, vt_scr, g_scr, u_scr, s_scr, x_scr, *, nseq, seq):
    c = SSM_CHUNK
    sub = c // SUBLANES
    gpb = GROUPS_PER_BLOCK
    npair = gpb // 2
    width = c * SSM_GROUP
    sp = npair * LANES
    rows8 = nseq * seq // SUBLANES
    rows = rows8 // sub
    nrows = seq // c

    @pl.when(pl.program_id(1) == 0)
    def _expand():
        slot = lax.broadcasted_iota(jnp.int32, (SSM_GROUP, width), 1) >> 4
        for g in range(gpb):
            pair, gi = divmod(g, 2)
            zt = ([], [])
            for d in range(2):
                b_re, b_im = bt_ref[d, g, 0], bt_ref[d, g, 1]
                c_re, c_im = ct_ref[d, g, 0], ct_ref[d, g, 1]
                for s in range(c):
                    r16 = slice(gi * width + s * SSM_GROUP, gi * width + (s + 1) * SSM_GROUP)
                    re_cols = slice((2 * d) * LANES, (2 * d + 1) * LANES)
                    im_cols = slice((2 * d + 1) * LANES, (2 * d + 2) * LANES)
                    p_w = c - 1 - s if d == 0 else s
                    p_v = s + 1 if d == 0 else c - s
                    p_z = s if d == 0 else c - 1 - s
                    wr, wi = _cmul(b_re, b_im, pw_ref[p_w, d, pair, 0], pw_ref[p_w, d, pair, 1])
                    w_scr[pair, r16, re_cols] = wr.astype(BF16)
                    w_scr[pair, r16, im_cols] = wi.astype(BF16)
                    er, ei = _cmul(c_re, c_im, pw_ref[p_v, d, pair, 0], pw_ref[p_v, d, pair, 1])
                    vt_scr[pair, r16, re_cols] = er.astype(BF16)
                    vt_scr[pair, r16, im_cols] = (-ei).astype(BF16)
                    er, ei = _cmul(c_re, c_im, pw_ref[p_z, d, pair, 0], pw_ref[p_z, d, pair, 1])
                    zt[d].append(jnp.concatenate([er, -ei], axis=1))
            bcat = [jnp.concatenate([bt_ref[d, g, 0], bt_ref[d, g, 1]], axis=1) for d in range(2)]
            f0 = _split_dot(bcat[0], jnp.concatenate(zt[0], axis=0)) + dt0_ref[g]
            b0 = _split_dot(bcat[1], jnp.concatenate(zt[1], axis=0))
            for s in range(c):
                fwd = jnp.where(slot >= s, _roll_lanes(f0, SSM_GROUP * s), 0.0)
                bwd = jnp.where(slot <= s, _roll_lanes(b0, SSM_GROUP * (s + 1)), 0.0)
                m_scr[g, s * SSM_GROUP:(s + 1) * SSM_GROUP, :] = (fwd + bwd).astype(BF16)

    ucat = jnp.concatenate(
        [u_ref[pl.ds(t, rows8, stride=SUBLANES), :].astype(BF16) for t in range(SUBLANES)], axis=1)
    grouped = jnp.dot(ucat, perm_ref[...], preferred_element_type=F32)
    for g in range(gpb):
        g_scr[g] = grouped[:, g * LANES:(g + 1) * LANES]
    for g in range(gpb):
        u_scr[g] = jnp.concatenate(
            [g_scr[g, pl.ds(k, rows, stride=sub), :] for k in range(sub)], axis=1).astype(BF16)

    for pair in range(npair):
        upair = jnp.concatenate([u_scr[2 * pair], u_scr[2 * pair + 1]], axis=1)
        sg = jnp.dot(upair, w_scr[pair], preferred_element_type=F32)
        for k in range(4):
            s_scr[:, k * sp + pair * LANES:k * sp + (pair + 1) * LANES] = sg[:, k * LANES:(k + 1) * LANES]

    nblk = nrows // SUBLANES
    row = lax.broadcasted_iota(jnp.int32, (SUBLANES, LANES), 0)

    def scan_block(pair, r0, d, carry):
        rows_ = slice(r0, r0 + SUBLANES)
        lanes_ = slice(pair * LANES, (pair + 1) * LANES)
        re_cols = slice(2 * d * sp + pair * LANES, 2 * d * sp + (pair + 1) * LANES)
        im_cols = slice((2 * d + 1) * sp + pair * LANES, (2 * d + 1) * sp + (pair + 1) * LANES)
        pr = s_scr[rows_, re_cols]
        pi = s_scr[rows_, im_cols]
        for step, sh in enumerate((1, 2, 4)):
            shift = sh if d == 0 else SUBLANES - sh
            tr, ti = _cmul(tab_ref[d, step, 0, :, lanes_], tab_ref[d, step, 1, :, lanes_],
                           pltpu.roll(pr, shift, 0), pltpu.roll(pi, shift, 0))
            pr = pr + tr
            pi = pi + ti
        cr, ci = carry
        tr, ti = _cmul(tab_ref[d, 3, 0, :, lanes_], tab_ref[d, 3, 1, :, lanes_], cr, ci)
        xr = pr + tr
        xi = pi + ti
        edge = 0 if d == 0 else SUBLANES - 1
        shift = 1 if d == 0 else SUBLANES - 1
        x_scr[rows_, re_cols] = jnp.where(row == edge, cr, pltpu.roll(xr, shift, 0))
        x_scr[rows_, im_cols] = jnp.where(row == edge, ci, pltpu.roll(xi, shift, 0))
        last = SUBLANES - 1 if d == 0 else 0
        return (jnp.broadcast_to(xr[last:last + 1, :], (SUBLANES, LANES)),
                jnp.broadcast_to(xi[last:last + 1, :], (SUBLANES, LANES)))

    zero = jnp.zeros((SUBLANES, LANES), F32)

    for pair in range(npair):
        for sq in range(nseq):
            cf = cb = (zero, zero)
            for i in range(nblk):
                cf = scan_block(pair, sq * nrows + i * SUBLANES, 0, cf)
                cb = scan_block(pair, sq * nrows + (nblk - 1 - i) * SUBLANES, 1, cb)
        xpair = jnp.concatenate(
            [x_scr[:, k * sp + pair * LANES:k * sp + (pair + 1) * LANES] for k in range(4)],
            axis=1).astype(BF16)
        cross = lax.dot_general(xpair, vt_scr[pair], (((1,), (1,)), ((), ())),
                                preferred_element_type=F32)
        for gi in range(2):
            g = 2 * pair + gi
            yg = (jnp.dot(u_scr[g], m_scr[g], preferred_element_type=F32)
                  + cross[:, gi * width:(gi + 1) * width])
            for k in range(sub):
                g_scr[g, pl.ds(k, rows, stride=sub), :] = yg[:, k * LANES:(k + 1) * LANES]

    ycat = jnp.concatenate([g_scr[g] for g in range(gpb)], axis=1).astype(BF16)
    ytok = jnp.dot(ycat, perm_ref[...], preferred_element_type=F32)
    for t in range(SUBLANES):
        y_ref[pl.ds(t, rows8, stride=SUBLANES), :] = ytok[:, t * LANES:(t + 1) * LANES]


def _s5(u_blocks, mats, seq):
    perm, bt, ct, pw, dt0, tabs = mats
    nb, t, _ = u_blocks.shape
    c = SSM_CHUNK
    gpb = GROUPS_PER_BLOCK
    npair = gpb // 2
    nseq = SSM_SEQS
    while (t // seq) % nseq:
        nseq //= 2
    rows8 = nseq * seq // SUBLANES
    rows = nseq * seq // c
    width = c * SSM_GROUP
    sp = npair * LANES
    return pl.pallas_call(
        functools.partial(_s5_kernel, nseq=nseq, seq=seq),
        grid=(nb, t // (nseq * seq)),
        in_specs=[
            pl.BlockSpec((None, nseq * seq, LANES), lambda b, i: (b, i, 0)),
            _const_spec((SUBLANES * LANES, SUBLANES * LANES)),
            pl.BlockSpec((None, 2, gpb, 2, SSM_GROUP, LANES), lambda b, i: (b, 0, 0, 0, 0, 0)),
            pl.BlockSpec((None, 2, gpb, 2, SSM_GROUP, LANES), lambda b, i: (b, 0, 0, 0, 0, 0)),
            pl.BlockSpec((None, c + 1, 2, npair, 2, 1, LANES),
                         lambda b, i: (b, 0, 0, 0, 0, 0, 0)),
            pl.BlockSpec((None, gpb, SSM_GROUP, width), lambda b, i: (b, 0, 0, 0)),
            pl.BlockSpec((None, 2, 4, 2, SUBLANES, sp), lambda b, i: (b, 0, 0, 0, 0, 0)),
        ],
        out_specs=pl.BlockSpec((None, nseq * seq, LANES), lambda b, i: (b, i, 0)),
        out_shape=jax.ShapeDtypeStruct((nb, t, LANES), F32),
        scratch_shapes=[pltpu.VMEM((gpb, width, width), BF16),
                        pltpu.VMEM((npair, 2 * width, 4 * LANES), BF16),
                        pltpu.VMEM((npair, 2 * width, 4 * LANES), BF16),
                        pltpu.VMEM((gpb, rows8, LANES), F32),
                        pltpu.VMEM((gpb, rows, width), BF16),
                        pltpu.VMEM((rows, 4 * sp), F32),
                        pltpu.VMEM((rows, 4 * sp), F32)],
        compiler_params=pltpu.CompilerParams(
            dimension_semantics=("arbitrary", "arbitrary"), vmem_limit_bytes=VMEM_LIMIT),
        name="s5",
    )(u_blocks, perm, bt, ct, pw, dt0, tabs)


def _s5_tables(lam_re, lam_im, log_dt, b_re, b_im, c_re, c_im, d_skip):
    c = SSM_CHUNK
    nb, gpb = SSM_BLOCKS, GROUPS_PER_BLOCK
    npair = gpb // 2
    lr = jnp.minimum(lam_re.astype(F32), -1e-4)
    li = lam_im.astype(F32)
    dt = jnp.exp(log_dt.astype(F32))[..., None]
    plist = list(range(c + 1)) + [c * k for k in range(2, SUBLANES + 1)]
    pidx = {pw_: i for i, pw_ in enumerate(plist)}
    p = jnp.asarray(plist, F32)[:, None, None, None]
    pmag = jnp.exp(p * (lr * dt)[None])
    pr = pmag * jnp.cos(p * (li * dt)[None])
    pi = pmag * jnp.sin(p * (li * dt)[None])
    ar, ai = pr[1], pi[1]
    den = lr * lr + li * li
    nr = ar - 1.0
    ni = ai
    cr = ((nr * lr + ni * li) / den)[:, :, None, :]
    ci = ((ni * lr - nr * li) / den)[:, :, None, :]
    b_re_t = b_re.astype(F32).transpose(0, 1, 3, 2)
    b_im_t = b_im.astype(F32).transpose(0, 1, 3, 2)
    bbar = jnp.stack([cr * b_re_t - ci * b_im_t, cr * b_im_t + ci * b_re_t], axis=2)
    cmat = jnp.stack([c_re.astype(F32), c_im.astype(F32)], axis=2)

    def half_placed(x):
        own = (jnp.arange(SSM_GROUPS) % 2)[:, None] == jnp.arange(2)[None, :]
        wide = jnp.where(own[None, :, None, None, :, None], x[:, :, :, :, None, :], 0.0)
        wide = wide.reshape(2, nb, gpb, 2, SSM_GROUP, LANES)
        return jnp.moveaxis(wide, 1, 0)

    pows = jnp.stack([pr, pi], axis=3)[:c + 1]
    pw = pows.reshape(c + 1, 2, nb, npair, 2, 2, SSM_STATE).transpose(2, 0, 1, 3, 5, 4, 6)
    pw = pw.reshape(nb, c + 1, 2, npair, 2, 1, LANES)

    lane = jnp.arange(c * SSM_GROUP)
    dt0 = jnp.where(lane[None, None, :] == jnp.arange(SSM_GROUP)[None, :, None],
                    d_skip.astype(F32).reshape(SSM_GROUPS, SSM_GROUP, 1), 0.0)
    dt0 = dt0.reshape(nb, gpb, SSM_GROUP, c * SSM_GROUP)

    rowi = jnp.arange(SUBLANES)

    def lanes(x):
        return x.reshape(x.shape[:-2] + (nb, npair * LANES))

    tabs = []
    for d in range(2):
        per_dir = []
        for sh in (1, 2, 4):
            keep = (rowi >= sh) if d == 0 else (rowi < SUBLANES - sh)
            per_dir.append(jnp.stack([
                jnp.where(keep[:, None, None], lanes(pr[pidx[sh * c], d])[None], 0.0),
                jnp.where(keep[:, None, None], lanes(pi[pidx[sh * c], d])[None], 0.0)]))
        mult = range(1, SUBLANES + 1) if d == 0 else range(SUBLANES, 0, -1)
        cpow = jnp.asarray([pidx[k * c] for k in mult])
        per_dir.append(jnp.stack([lanes(pr[cpow, d]), lanes(pi[cpow, d])]))
        tabs.append(jnp.stack(per_dir))
    tabs = jnp.stack(tabs).transpose(4, 0, 1, 2, 3, 5)

    src = jnp.arange(SUBLANES * LANES)
    dst = ((src >> 4) & 7) * LANES + (src >> 7) * SSM_GROUP + (src & 15)
    perm = (dst[:, None] == src[None, :]).astype(BF16)
    return perm, half_placed(bbar), half_placed(cmat), pw, dt0, tabs


def _mix_norm_kernel(x_ref, ret_ref, ssm_ref, glu_w_ref, glu_b_ref, w_out_ref, nffn_ref,
                     mix_ref, h_ref):
    ssm = jnp.concatenate([ssm_ref[b] for b in range(SSM_BLOCKS)], axis=1)
    y = jax.nn.gelu(ssm)
    gate = jax.nn.sigmoid(
        jnp.dot(y.astype(BF16), glu_w_ref[...], preferred_element_type=F32) + glu_b_ref[...])
    cat = jnp.concatenate([ret_ref[...], (y * gate).astype(BF16)], axis=1)
    mix = jnp.dot(cat, w_out_ref[...], preferred_element_type=F32).astype(BF16)
    mix_ref[...] = mix
    h_ref[...] = _rms(x_ref[...] + mix.astype(F32), nffn_ref[...]).astype(BF16)


def _mix_norm(x2, ret, ssm, glu_w, glu_b, w_out, norm_ffn):
    t = x2.shape[0]
    r = MIX_TILE
    tok = lambda i: (i, 0)
    return pl.pallas_call(
        _mix_norm_kernel,
        grid=(t // r,),
        in_specs=[pl.BlockSpec((r, D_MODEL), tok), pl.BlockSpec((r, RET_WIDTH), tok),
                  pl.BlockSpec((SSM_BLOCKS, r, LANES), lambda i: (0, i, 0)),
                  _const_spec((SSM_WIDTH, SSM_WIDTH)), _const_spec((1, SSM_WIDTH)),
                  _const_spec((D_MODEL, D_MODEL)), _const_spec((1, D_MODEL))],
        out_specs=[pl.BlockSpec((r, D_MODEL), tok), pl.BlockSpec((r, D_MODEL), tok)],
        out_shape=[jax.ShapeDtypeStruct((t, D_MODEL), BF16),
                   jax.ShapeDtypeStruct((t, D_MODEL), BF16)],
        compiler_params=pltpu.CompilerParams(
            dimension_semantics=("arbitrary",), vmem_limit_bytes=VMEM_LIMIT),
        name="mix_norm",
    )(x2, ret, ssm, glu_w, glu_b, w_out, norm_ffn)


def _conv_ffn_kernel(x_ref, mix_ref, hm_ref, hp_ref, hn_ref, w_up_ref, cw_ref, cb_ref,
                     w_down_ref, nfin_ref, o_ref, hid_scr, *, tiles_per_seq):
    r = hm_ref.shape[0]
    ext = r + 2 * HALO
    j = pl.program_id(0) % tiles_per_seq
    hp = jnp.where(j == 0, jnp.zeros(hp_ref.shape, BF16), hp_ref[...])
    hn = jnp.where(j == tiles_per_seq - 1, jnp.zeros(hn_ref.shape, BF16), hn_ref[...])
    h = jnp.concatenate([hp, hm_ref[...], hn], axis=0)
    for jj in range(D_FF // FF_TILE):
        parts = []
        for off in (jj * FF_TILE, D_FF + jj * FF_TILE):
            z = jnp.dot(h, w_up_ref[:, off:off + FF_TILE], preferred_element_type=F32)
            zc = (pltpu.roll(z, 1, 0)[HALO:HALO + r] * cw_ref[0:1, off:off + FF_TILE]
                  + z[HALO:HALO + r] * cw_ref[1:2, off:off + FF_TILE]
                  + pltpu.roll(z, ext - 1, 0)[HALO:HALO + r] * cw_ref[2:3, off:off + FF_TILE]
                  + cb_ref[:, off:off + FF_TILE])
            parts.append(zc)
        val, gate = parts
        t = jnp.tanh(gate * (GELU_C0 + (GELU_C0 * GELU_C1) * (gate * gate)))
        hid_scr[:, jj * FF_TILE:(jj + 1) * FF_TILE] = ((0.5 * gate * val) * (1.0 + t)).astype(BF16)
    x1 = x_ref[...] + mix_ref[...].astype(F32)
    x2 = x1 + jnp.dot(hid_scr[...], w_down_ref[...], preferred_element_type=F32)
    o_ref[...] = _rms(x2, nfin_ref[...])


def _conv_ffn(x, mix, h, w_up, conv_w, conv_b, w_down, norm_final, seq):
    t = x.shape[0]
    r = FFN_TILE
    assert seq % r == 0, "token tiles must not straddle sequences"
    tiles_per_seq = seq // r
    hb = r // HALO
    last = t // HALO - 1
    tok = lambda i: (i, 0)
    return pl.pallas_call(
        functools.partial(_conv_ffn_kernel, tiles_per_seq=tiles_per_seq),
        grid=(t // r,),
        in_specs=[pl.BlockSpec((r, D_MODEL), tok), pl.BlockSpec((r, D_MODEL), tok),
                  pl.BlockSpec((r, D_MODEL), tok),
                  pl.BlockSpec((HALO, D_MODEL), lambda i: (jnp.maximum(i * hb - 1, 0), 0)),
                  pl.BlockSpec((HALO, D_MODEL), lambda i: (jnp.minimum((i + 1) * hb, last), 0)),
                  _const_spec((D_MODEL, 2 * D_FF)), _const_spec((3, 2 * D_FF)),
                  _const_spec((1, 2 * D_FF)), _const_spec((D_FF, D_MODEL)),
                  _const_spec((1, D_MODEL))],
        out_specs=pl.BlockSpec((r, D_MODEL), tok),
        out_shape=jax.ShapeDtypeStruct((t, D_MODEL), F32),
        scratch_shapes=[pltpu.VMEM((r, D_FF), BF16)],
        compiler_params=pltpu.CompilerParams(
            dimension_semantics=("arbitrary",), vmem_limit_bytes=VMEM_LIMIT),
        name="conv_ffn",
    )(x, mix, h, h, h, w_up, conv_w, conv_b, w_down, norm_final)


def _rotary_tables(seq):
    half = HEAD_DIM // 2
    inv_freq = ROPE_BASE ** (-jnp.arange(half, dtype=F32) / half)
    ang = jnp.arange(seq, dtype=F32)[:, None] * inv_freq[None, :]
    cos = jnp.cos(ang)
    sin = jnp.sin(ang)
    return jnp.concatenate([cos, cos], axis=1), jnp.concatenate([-sin, sin], axis=1)


def _layer(x, prep):
    b, seq, _ = x.shape
    x2 = x.reshape(b * seq, D_MODEL)
    q, k, v, g, u = _in_proj(x2, prep["norm_mix"], prep["w_in"], prep["cos"], prep["sin"],
                             prep["gn_gain"], seq)
    ret = _retention(q, k, v, g, prep["ret_tabs"], seq)
    ssm = _s5(u, prep["s5"], seq)
    mix, h = _mix_norm(x2, ret, ssm, prep["glu_w"], prep["glu_b"], prep["w_out"], prep["norm_ffn"])
    out = _conv_ffn(x2, mix, h, prep["w_up"], prep["conv_w"], prep["conv_b"], prep["w_down"],
                    prep["norm_final"], seq)
    return out.reshape(b, seq, D_MODEL)


def kernel(x_prompt, x_sample, norm_mix, w_in, ret_gn_gain, s5_lambda_re, s5_lambda_im, s5_log_dt, s5_B_re, s5_B_im, s5_C_re, s5_C_im, s5_D, s5_glu_w, s5_glu_b, w_out, norm_ffn, w_up, conv_w, conv_b, w_down, norm_final):
    assert norm_mix.shape[0] == 1, "single-layer trunk"
    seq = x_prompt.shape[1]
    assert x_sample.shape[1] == seq
    cos, sin = _rotary_tables(seq)
    prep = {
        "norm_mix": norm_mix[0][None].astype(F32),
        "w_in": w_in[0].astype(BF16),
        "cos": cos, "sin": sin,
        "ret_tabs": _retention_tables(),
        "gn_gain": ret_gn_gain[0][None].astype(F32),
        "s5": _s5_tables(s5_lambda_re[0], s5_lambda_im[0], s5_log_dt[0], s5_B_re[0], s5_B_im[0],
                         s5_C_re[0], s5_C_im[0], s5_D[0]),
        "glu_w": s5_glu_w[0].astype(BF16),
        "glu_b": s5_glu_b[0][None].astype(F32),
        "w_out": w_out[0].astype(BF16),
        "norm_ffn": norm_ffn[0][None].astype(F32),
        "w_up": w_up[0].astype(BF16),
        "conv_w": conv_w[0].astype(F32),
        "conv_b": conv_b[0][None].astype(F32),
        "w_down": w_down[0].astype(BF16),
        "norm_final": norm_final[None].astype(F32),
    }
    return (_layer(x_prompt, prep), _layer(x_sample, prep))
```

```python
import functools

import jax
import jax.numpy as jnp
from jax import lax
from jax.experimental import pallas as pl
from jax.experimental.pallas import tpu as pltpu

D_MODEL = 1024
RET_WIDTH = 512
SSM_WIDTH = 512
RET_HEADS = 4
HEAD_DIM = 128
ROPE_BASE = 10000.0
SSM_GROUP = 16
SSM_GROUPS = 32
SSM_STATE = 64
D_FF = 2816
EPS = 1e-6
GELU_C0 = 0.7978845608028654
GELU_C1 = 0.044715
IN_WIDTH = 4 * RET_WIDTH + SSM_WIDTH

LANES = 128
SUBLANES = 8
BF16_ROWS = 16
VMEM_LIMIT = 56 * 1024 * 1024

RET_BLOCK = 256
SSM_BLOCKS = SSM_WIDTH // LANES
GROUPS_PER_BLOCK = LANES // SSM_GROUP
SSM_CHUNK = 16
SSM_SEQS = 4
TOK_TILE = 1024
FFN_TILE = 1024
MIX_TILE = 1024
FF_TILE = 256
HALO = BF16_ROWS

BF16 = jnp.bfloat16
F32 = jnp.float32


def _rms(x, w):
    return x * lax.rsqrt(jnp.mean(x * x, axis=-1, keepdims=True) + EPS) * w


def _const_spec(shape):
    nd = len(shape)
    return pl.BlockSpec(shape, lambda *_: (0,) * nd, pipeline_mode=pl.Buffered(1))


def _in_proj_kernel(x_ref, nw_ref, w_ref, cos_ref, sin_ref, gain_ref,
                    q_ref, k_ref, v_ref, g_ref, u_ref):
    h = _rms(x_ref[...], nw_ref[...]).astype(BF16)
    proj = jnp.dot(h, w_ref[...], preferred_element_type=F32)
    cos = cos_ref[...]
    sin = sin_ref[...]
    kscale = HEAD_DIM ** -0.5
    for hd in range(RET_HEADS):
        lo = hd * HEAD_DIM
        qh = proj[:, lo:lo + HEAD_DIM]
        q_ref[:, lo:lo + HEAD_DIM] = (qh * cos + pltpu.roll(qh, HEAD_DIM // 2, 1) * sin).astype(BF16)
        kh = proj[:, RET_WIDTH + lo:RET_WIDTH + lo + HEAD_DIM]
        kr = kh * cos + pltpu.roll(kh, HEAD_DIM // 2, 1) * sin
        k_ref[:, lo:lo + HEAD_DIM] = (kr * kscale).astype(BF16)
    v_ref[...] = proj[:, 2 * RET_WIDTH:3 * RET_WIDTH].astype(BF16)
    g_ref[...] = (jax.nn.silu(proj[:, 3 * RET_WIDTH:4 * RET_WIDTH]) * gain_ref[...]).astype(BF16)
    for blk in range(SSM_BLOCKS):
        lo = 4 * RET_WIDTH + blk * LANES
        u_ref[blk] = proj[:, lo:lo + LANES]


def _in_proj(x2, norm_w, w_in, cos2, sin2, gain, seq):
    t = x2.shape[0]
    r = TOK_TILE
    assert seq % r == 0, "token tiles must not straddle sequences"
    tiles_per_seq = seq // r
    tok = lambda i: (i, 0)
    out_tok = jax.ShapeDtypeStruct((t, RET_WIDTH), BF16)
    return pl.pallas_call(
        _in_proj_kernel,
        grid=(t // r,),
        in_specs=[
            pl.BlockSpec((r, D_MODEL), tok),
            _const_spec((1, D_MODEL)),
            _const_spec((D_MODEL, IN_WIDTH)),
            pl.BlockSpec((r, HEAD_DIM), lambda i: (i % tiles_per_seq, 0)),
            pl.BlockSpec((r, HEAD_DIM), lambda i: (i % tiles_per_seq, 0)),
            _const_spec((1, RET_WIDTH)),
        ],
        out_specs=[
            pl.BlockSpec((r, RET_WIDTH), tok),
            pl.BlockSpec((r, RET_WIDTH), tok),
            pl.BlockSpec((r, RET_WIDTH), tok),
            pl.BlockSpec((r, RET_WIDTH), tok),
            pl.BlockSpec((SSM_BLOCKS, r, LANES), lambda i: (0, i, 0)),
        ],
        out_shape=[out_tok, out_tok, out_tok, out_tok,
                   jax.ShapeDtypeStruct((SSM_BLOCKS, t, LANES), F32)],
        compiler_params=pltpu.CompilerParams(
            dimension_semantics=("arbitrary",), vmem_limit_bytes=VMEM_LIMIT),
        name="in_proj",
    )(x2, norm_w, w_in, cos2, sin2, gain)


def _retention_kernel(q_ref, k_ref, v_ref, g_ref, dmat_ref, cd_ref, row_ref, o_ref,
                      kv_scr, r_scr):
    c = RET_BLOCK
    nchunks = q_ref.shape[0] // c
    heads = [slice(hd * HEAD_DIM, (hd + 1) * HEAD_DIM) for hd in range(RET_HEADS)]

    for n in range(nchunks):
        rows = slice(n * c, (n + 1) * c)
        for hd, cols in enumerate(heads):
            kn = k_ref[rows, cols]
            kcat = jnp.concatenate([kn * row_ref[hd, 0], kn * row_ref[hd, 1]], axis=1)
            kv_scr[n, hd] = lax.dot_general(kcat, v_ref[rows, cols], (((0,), (0,)), ((), ())),
                                            preferred_element_type=F32)

    for hd in range(RET_HEADS):
        cd = cd_ref[hd]
        rf = jnp.zeros((HEAD_DIM, HEAD_DIM), F32)
        for n in range(nchunks):
            r_scr[n, hd, 0:HEAD_DIM, :] = rf.astype(BF16)
            rf = cd * rf + kv_scr[n, hd, 0:HEAD_DIM, :]
        rb = jnp.zeros((HEAD_DIM, HEAD_DIM), F32)
        for n in reversed(range(nchunks)):
            r_scr[n, hd, HEAD_DIM:2 * HEAD_DIM, :] = rb.astype(BF16)
            rb = cd * rb + kv_scr[n, hd, HEAD_DIM:2 * HEAD_DIM, :]

    for n in range(nchunks):
        rows = slice(n * c, (n + 1) * c)
        for hd, cols in enumerate(heads):
            qn = q_ref[rows, cols]
            s = lax.dot_general(qn, k_ref[rows, cols], (((1,), (1,)), ((), ())),
                                preferred_element_type=F32) * dmat_ref[hd]
            inner = jnp.dot(s.astype(BF16), v_ref[rows, cols], preferred_element_type=F32)
            qcat = jnp.concatenate([qn * row_ref[hd, 2], qn * row_ref[hd, 3]], axis=1)
            o = inner + jnp.dot(qcat, r_scr[n, hd], preferred_element_type=F32)
            o = o * lax.rsqrt(jnp.mean(o * o, axis=-1, keepdims=True) + EPS)
            o_ref[rows, cols] = (g_ref[rows, cols].astype(F32) * o).astype(BF16)


def _retention(q, k, v, g, tabs, seq):
    dmat, cd, tab_rows = tabs
    t = q.shape[0]
    nchunks = seq // RET_BLOCK
    blk = pl.BlockSpec((seq, RET_WIDTH), lambda b: (b, 0))
    return pl.pallas_call(
        _retention_kernel,
        grid=(t // seq,),
        in_specs=[blk, blk, blk, blk,
                  _const_spec((RET_HEADS, RET_BLOCK, RET_BLOCK)),
                  _const_spec((RET_HEADS, HEAD_DIM, HEAD_DIM)),
                  _const_spec((RET_HEADS, 4, RET_BLOCK, HEAD_DIM))],
        out_specs=blk,
        out_shape=jax.ShapeDtypeStruct((t, RET_WIDTH), BF16),
        scratch_shapes=[pltpu.VMEM((nchunks, RET_HEADS, 2 * HEAD_DIM, HEAD_DIM), F32),
                        pltpu.VMEM((nchunks, RET_HEADS, 2 * HEAD_DIM, HEAD_DIM), BF16)],
        compiler_params=pltpu.CompilerParams(
            dimension_semantics=("arbitrary",), vmem_limit_bytes=VMEM_LIMIT),
        name="retention",
    )(q, k, v, g, dmat, cd, tab_rows)


def _retention_tables():
    c = RET_BLOCK
    lg = jnp.log(1.0 - 2.0 ** (-5.0 - jnp.arange(RET_HEADS, dtype=F32)))[:, None, None]
    idx = jnp.arange(c, dtype=F32)
    col = jnp.broadcast_to(idx[:, None], (c, HEAD_DIM))[None]
    dmat = jnp.exp(lg * jnp.abs(idx[:, None] - idx[None, :])[None])
    cd = jnp.exp(lg * c) * jnp.ones((1, HEAD_DIM, HEAD_DIM), F32)
    zf = jnp.exp(lg * (c - 1.0 - col))
    zb = jnp.exp(lg * col)
    xf = jnp.exp(lg * (col + 1.0))
    xb = jnp.exp(lg * (c - col))
    return dmat, cd, jnp.stack([zf, zb, xf, xb], axis=1).astype(BF16)


def _cmul(ar, ai, br, bi):
    return ar * br - ai * bi, ar * bi + ai * br


def _roll_lanes(x, shift):
    ntiles = x.shape[1] // LANES
    whole, part = divmod(shift % x.shape[1], LANES)
    tiles = [x[:, k * LANES:(k + 1) * LANES] for k in range(ntiles)]
    tiles = [tiles[(k - whole) % ntiles] for k in range(ntiles)]
    if part:
        lane = lax.broadcasted_iota(jnp.int32, tiles[0].shape, 1)
        rolled = [pltpu.roll(t, part, 1) for t in tiles]
        tiles = [jnp.where(lane < part, rolled[(k - 1) % ntiles], rolled[k]) for k in range(ntiles)]
    return tiles[0] if ntiles == 1 else jnp.concatenate(tiles, axis=1)


def _split_dot(a, z):
    dims = (((1,), (1,)), ((), ()))
    a_hi = a.astype(BF16)
    z_hi = z.astype(BF16)
    a_lo = (a - a_hi.astype(F32)).astype(BF16)
    z_lo = (z - z_hi.astype(F32)).astype(BF16)
    return (lax.dot_general(a_hi, z_hi, dims, preferred_element_type=F32)
            + lax.dot_general(a_hi, z_lo, dims, preferred_element_type=F32)
            + lax.dot_general(a_lo, z_hi, dims, preferred_element_type=F32))


def _s5_kernel(u_ref, perm_ref, bt_ref, ct_ref, pw_ref, dt0_ref, tab_ref, y_ref,
               m_scr, w_scr, vt_scr, g_scr, u_scr, s_scr, x_scr, *, nseq, seq):
    c = SSM_CHUNK
    sub = c // SUBLANES
    gpb = GROUPS_PER_BLOCK
    npair = gpb // 2
    width = c * SSM_GROUP
    sp = npair * LANES
    rows8 = nseq * seq // SUBLANES
    rows = rows8 // sub
    nrows = seq // c

    @pl.when(pl.program_id(1) == 0)
    def _expand():
        slot = lax.broadcasted_iota(jnp.int32, (SSM_GROUP, width), 1) >> 4
        for g in range(gpb):
            pair, gi = divmod(g, 2)
            zt = ([], [])
            for d in range(2):
                b_re, b_im = bt_ref[d, g, 0], bt_ref[d, g, 1]
                c_re, c_im = ct_ref[d, g, 0], ct_ref[d, g, 1]
                for s in range(c):
                    r16 = slice(gi * width + s * SSM_GROUP, gi * width + (s + 1) * SSM_GROUP)
                    re_cols = slice((2 * d) * LANES, (2 * d + 1) * LANES)
                    im_cols = slice((2 * d + 1) * LANES, (2 * d + 2) * LANES)
                    p_w = c - 1 - s if d == 0 else s
                    p_v = s + 1 if d == 0 else c - s
                    p_z = s if d == 0 else c - 1 - s
                    wr, wi = _cmul(b_re, b_im, pw_ref[p_w, d, pair, 0], pw_ref[p_w, d, pair, 1])
                    w_scr[pair, r16, re_cols] = wr.astype(BF16)
                    w_scr[pair, r16, im_cols] = wi.astype(BF16)
                    er, ei = _cmul(c_re, c_im, pw_ref[p_v, d, pair, 0], pw_ref[p_v, d, pair, 1])
                    vt_scr[pair, r16, re_cols] = er.astype(BF16)
                    vt_scr[pair, r16, im_cols] = (-ei).astype(BF16)
                    er, ei = _cmul(c_re, c_im, pw_ref[p_z, d, pair, 0], pw_ref[p_z, d, pair, 1])
                    zt[d].append(jnp.concatenate([er, -ei], axis=1))
            bcat = [jnp.concatenate([bt_ref[d, g, 0], bt_ref[d, g, 1]], axis=1) for d in range(2)]
            f0 = _split_dot(bcat[0], jnp.concatenate(zt[0], axis=0)) + dt0_ref[g]
            b0 = _split_dot(bcat[1], jnp.concatenate(zt[1], axis=0))
            for s in range(c):
                fwd = jnp.where(slot >= s, _roll_lanes(f0, SSM_GROUP * s), 0.0)
                bwd = jnp.where(slot <= s, _roll_lanes(b0, SSM_GROUP * (s + 1)), 0.0)
                m_scr[g, s * SSM_GROUP:(s + 1) * SSM_GROUP, :] = (fwd + bwd).astype(BF16)

    ucat = jnp.concatenate(
        [u_ref[pl.ds(t, rows8, stride=SUBLANES), :].astype(BF16) for t in range(SUBLANES)], axis=1)
    grouped = jnp.dot(ucat, perm_ref[...], preferred_element_type=F32)
    for g in range(gpb):
        g_scr[g] = grouped[:, g * LANES:(g + 1) * LANES]
    for g in range(gpb):
        u_scr[g] = jnp.concatenate(
            [g_scr[g, pl.ds(k, rows, stride=sub), :] for k in range(sub)], axis=1).astype(BF16)

    for pair in range(npair):
        upair = jnp.concatenate([u_scr[2 * pair], u_scr[2 * pair + 1]], axis=1)
        sg = jnp.dot(upair, w_scr[pair], preferred_element_type=F32)
        for k in range(4):
            s_scr[:, k * sp + pair * LANES:k * sp + (pair + 1) * LANES] = sg[:, k * LANES:(k + 1) * LANES]

    nblk = nrows // SUBLANES
    row = lax.broadcasted_iota(jnp.int32, (SUBLANES, LANES), 0)

    def scan_block(pair, r0, d, carry):
        rows_ = slice(r0, r0 + SUBLANES)
        lanes_ = slice(pair * LANES, (pair + 1) * LANES)
        re_cols = slice(2 * d * sp + pair * LANES, 2 * d * sp + (pair + 1) * LANES)
        im_cols = slice((2 * d + 1) * sp + pair * LANES, (2 * d + 1) * sp + (pair + 1) * LANES)
        pr = s_scr[rows_, re_cols]
        pi = s_scr[rows_, im_cols]
        for step, sh in enumerate((1, 2, 4)):
            shift = sh if d == 0 else SUBLANES - sh
            tr, ti = _cmul(tab_ref[d, step, 0, :, lanes_], tab_ref[d, step, 1, :, lanes_],
                           pltpu.roll(pr, shift, 0), pltpu.roll(pi, shift, 0))
            pr = pr + tr
            pi = pi + ti
        cr, ci = carry
        tr, ti = _cmul(tab_ref[d, 3, 0, :, lanes_], tab_ref[d, 3, 1, :, lanes_], cr, ci)
        xr = pr + tr
        xi = pi + ti
        edge = 0 if d == 0 else SUBLANES - 1
        shift = 1 if d == 0 else SUBLANES - 1
        x_scr[rows_, re_cols] = jnp.where(row == edge, cr, pltpu.roll(xr, shift, 0))
        x_scr[rows_, im_cols] = jnp.where(row == edge, ci, pltpu.roll(xi, shift, 0))
        last = SUBLANES - 1 if d == 0 else 0
        return (jnp.broadcast_to(xr[last:last + 1, :], (SUBLANES, LANES)),
                jnp.broadcast_to(xi[last:last + 1, :], (SUBLANES, LANES)))

    zero = jnp.zeros((SUBLANES, LANES), F32)

    for pair in range(npair):
        for sq in range(nseq):
            cf = cb = (zero, zero)
            for i in range(nblk):
                cf = scan_block(pair, sq * nrows + i * SUBLANES, 0, cf)
                cb = scan_block(pair, sq * nrows + (nblk - 1 - i) * SUBLANES, 1, cb)
        xpair = jnp.concatenate(
            [x_scr[:, k * sp + pair * LANES:k * sp + (pair + 1) * LANES] for k in range(4)],
            axis=1).astype(BF16)
        cross = lax.dot_general(xpair, vt_scr[pair], (((1,), (1,)), ((), ())),
                                preferred_element_type=F32)
        for gi in range(2):
            g = 2 * pair + gi
            yg = (jnp.dot(u_scr[g], m_scr[g], preferred_element_type=F32)
                  + cross[:, gi * width:(gi + 1) * width])
            for k in range(sub):
                g_scr[g, pl.ds(k, rows, stride=sub), :] = yg[:, k * LANES:(k + 1) * LANES]

    ycat = jnp.concatenate([g_scr[g] for g in range(gpb)], axis=1).astype(BF16)
    ytok = jnp.dot(ycat, perm_ref[...], preferred_element_type=F32)
    for t in range(SUBLANES):
        y_ref[pl.ds(t, rows8, stride=SUBLANES), :] = ytok[:, t * LANES:(t + 1) * LANES]


def _s5(u_blocks, mats, seq):
    perm, bt, ct, pw, dt0, tabs = mats
    nb, t, _ = u_blocks.shape
    c = SSM_CHUNK
    gpb = GROUPS_PER_BLOCK
    npair = gpb // 2
    nseq = SSM_SEQS
    while (t // seq) % nseq:
        nseq //= 2
    rows8 = nseq * seq // SUBLANES
    rows = nseq * seq // c
    width = c * SSM_GROUP
    sp = npair * LANES
    return pl.pallas_call(
        functools.partial(_s5_kernel, nseq=nseq, seq=seq),
        grid=(nb, t // (nseq * seq)),
        in_specs=[
            pl.BlockSpec((None, nseq * seq, LANES), lambda b, i: (b, i, 0)),
            _const_spec((SUBLANES * LANES, SUBLANES * LANES)),
            pl.BlockSpec((None, 2, gpb, 2, SSM_GROUP, LANES), lambda b, i: (b, 0, 0, 0, 0, 0)),
            pl.BlockSpec((None, 2, gpb, 2, SSM_GROUP, LANES), lambda b, i: (b, 0, 0, 0, 0, 0)),
            pl.BlockSpec((None, c + 1, 2, npair, 2, 1, LANES),
                         lambda b, i: (b, 0, 0, 0, 0, 0, 0)),
            pl.BlockSpec((None, gpb, SSM_GROUP, width), lambda b, i: (b, 0, 0, 0)),
            pl.BlockSpec((None, 2, 4, 2, SUBLANES, sp), lambda b, i: (b, 0, 0, 0, 0, 0)),
        ],
        out_specs=pl.BlockSpec((None, nseq * seq, LANES), lambda b, i: (b, i, 0)),
        out_shape=jax.ShapeDtypeStruct((nb, t, LANES), F32),
        scratch_shapes=[pltpu.VMEM((gpb, width, width), BF16),
                        pltpu.VMEM((npair, 2 * width, 4 * LANES), BF16),
                        pltpu.VMEM((npair, 2 * width, 4 * LANES), BF16),
                        pltpu.VMEM((gpb, rows8, LANES), F32),
                        pltpu.VMEM((gpb, rows, width), BF16),
                        pltpu.VMEM((rows, 4 * sp), F32),
                        pltpu.VMEM((rows, 4 * sp), F32)],
        compiler_params=pltpu.CompilerParams(
            dimension_semantics=("arbitrary", "arbitrary"), vmem_limit_bytes=VMEM_LIMIT),
        name="s5",
    )(u_blocks, perm, bt, ct, pw, dt0, tabs)


def _s5_tables(lam_re, lam_im, log_dt, b_re, b_im, c_re, c_im, d_skip):
    c = SSM_CHUNK
    nb, gpb = SSM_BLOCKS, GROUPS_PER_BLOCK
    npair = gpb // 2
    lr = jnp.minimum(lam_re.astype(F32), -1e-4)
    li = lam_im.astype(F32)
    dt = jnp.exp(log_dt.astype(F32))[..., None]
    plist = list(range(c + 1)) + [c * k for k in range(2, SUBLANES + 1)]
    pidx = {pw_: i for i, pw_ in enumerate(plist)}
    p = jnp.asarray(plist, F32)[:, None, None, None]
    pmag = jnp.exp(p * (lr * dt)[None])
    pr = pmag * jnp.cos(p * (li * dt)[None])
    pi = pmag * jnp.sin(p * (li * dt)[None])
    ar, ai = pr[1], pi[1]
    den = lr * lr + li * li
    nr = ar - 1.0
    ni = ai
    cr = ((nr * lr + ni * li) / den)[:, :, None, :]
    ci = ((ni * lr - nr * li) / den)[:, :, None, :]
    b_re_t = b_re.astype(F32).transpose(0, 1, 3, 2)
    b_im_t = b_im.astype(F32).transpose(0, 1, 3, 2)
    bbar = jnp.stack([cr * b_re_t - ci * b_im_t, cr * b_im_t + ci * b_re_t], axis=2)
    cmat = jnp.stack([c_re.astype(F32), c_im.astype(F32)], axis=2)

    def half_placed(x):
        own = (jnp.arange(SSM_GROUPS) % 2)[:, None] == jnp.arange(2)[None, :]
        wide = jnp.where(own[None, :, None, None, :, None], x[:, :, :, :, None, :], 0.0)
        wide = wide.reshape(2, nb, gpb, 2, SSM_GROUP, LANES)
        return jnp.moveaxis(wide, 1, 0)

    pows = jnp.stack([pr, pi], axis=3)[:c + 1]
    pw = pows.reshape(c + 1, 2, nb, npair, 2, 2, SSM_STATE).transpose(2, 0, 1, 3, 5, 4, 6)
    pw = pw.reshape(nb, c + 1, 2, npair, 2, 1, LANES)

    lane = jnp.arange(c * SSM_GROUP)
    dt0 = jnp.where(lane[None, None, :] == jnp.arange(SSM_GROUP)[None, :, None],
                    d_skip.astype(F32).reshape(SSM_GROUPS, SSM_GROUP, 1), 0.0)
    dt0 = dt0.reshape(nb, gpb, SSM_GROUP, c * SSM_GROUP)

    rowi = jnp.arange(SUBLANES)

    def lanes(x):
        return x.reshape(x.shape[:-2] + (nb, npair * LANES))

    tabs = []
    for d in range(2):
        per_dir = []
        for sh in (1, 2, 4):
            keep = (rowi >= sh) if d == 0 else (rowi < SUBLANES - sh)
            per_dir.append(jnp.stack([
                jnp.where(keep[:, None, None], lanes(pr[pidx[sh * c], d])[None], 0.0),
                jnp.where(keep[:, None, None], lanes(pi[pidx[sh * c], d])[None], 0.0)]))
        mult = range(1, SUBLANES + 1) if d == 0 else range(SUBLANES, 0, -1)
        cpow = jnp.asarray([pidx[k * c] for k in mult])
        per_dir.append(jnp.stack([lanes(pr[cpow, d]), lanes(pi[cpow, d])]))
        tabs.append(jnp.stack(per_dir))
    tabs = jnp.stack(tabs).transpose(4, 0, 1, 2, 3, 5)

    src = jnp.arange(SUBLANES * LANES)
    dst = ((src >> 4) & 7) * LANES + (src >> 7) * SSM_GROUP + (src & 15)
    perm = (dst[:, None] == src[None, :]).astype(BF16)
    return perm, half_placed(bbar), half_placed(cmat), pw, dt0, tabs


def _mix_norm_kernel(x_ref, ret_ref, ssm_ref, glu_w_ref, glu_b_ref, w_out_ref, nffn_ref,
                     mix_ref, h_ref):
    ssm = jnp.concatenate([ssm_ref[b] for b in range(SSM_BLOCKS)], axis=1)
    y = jax.nn.gelu(ssm)
    gate = jax.nn.sigmoid(
        jnp.dot(y.astype(BF16), glu_w_ref[...], preferred_element_type=F32) + glu_b_ref[...])
    cat = jnp.concatenate([ret_ref[...], (y * gate).astype(BF16)], axis=1)
    mix = jnp.dot(cat, w_out_ref[...], preferred_element_type=F32).astype(BF16)
    mix_ref[...] = mix
    h_ref[...] = _rms(x_ref[...] + mix.astype(F32), nffn_ref[...]).astype(BF16)


def _mix_norm(x2, ret, ssm, glu_w, glu_b, w_out, norm_ffn):
    t = x2.shape[0]
    r = MIX_TILE
    tok = lambda i: (i, 0)
    return pl.pallas_call(
        _mix_norm_kernel,
        grid=(t // r,),
        in_specs=[pl.BlockSpec((r, D_MODEL), tok), pl.BlockSpec((r, RET_WIDTH), tok),
                  pl.BlockSpec((SSM_BLOCKS, r, LANES), lambda i: (0, i, 0)),
                  _const_spec((SSM_WIDTH, SSM_WIDTH)), _const_spec((1, SSM_WIDTH)),
                  _const_spec((D_MODEL, D_MODEL)), _const_spec((1, D_MODEL))],
        out_specs=[pl.BlockSpec((r, D_MODEL), tok), pl.BlockSpec((r, D_MODEL), tok)],
        out_shape=[jax.ShapeDtypeStruct((t, D_MODEL), BF16),
                   jax.ShapeDtypeStruct((t, D_MODEL), BF16)],
        compiler_params=pltpu.CompilerParams(
            dimension_semantics=("arbitrary",), vmem_limit_bytes=VMEM_LIMIT),
        name="mix_norm",
    )(x2, ret, ssm, glu_w, glu_b, w_out, norm_ffn)


def _conv_ffn_kernel(x_ref, mix_ref, hm_ref, hp_ref, hn_ref, w_up_ref, cw_ref, cb_ref,
                     w_down_ref, nfin_ref, o_ref, hid_scr, *, tiles_per_seq):
    r = hm_ref.shape[0]
    ext = r + 2 * HALO
    j = pl.program_id(0) % tiles_per_seq
    hp = jnp.where(j == 0, jnp.zeros(hp_ref.shape, BF16), hp_ref[...])
    hn = jnp.where(j == tiles_per_seq - 1, jnp.zeros(hn_ref.shape, BF16), hn_ref[...])
    h = jnp.concatenate([hp, hm_ref[...], hn], axis=0)
    for jj in range(D_FF // FF_TILE):
        parts = []
        for off in (jj * FF_TILE, D_FF + jj * FF_TILE):
            z = jnp.dot(h, w_up_ref[:, off:off + FF_TILE], preferred_element_type=F32)
            zc = (pltpu.roll(z, 1, 0)[HALO:HALO + r] * cw_ref[0:1, off:off + FF_TILE]
                  + z[HALO:HALO + r] * cw_ref[1:2, off:off + FF_TILE]
                  + pltpu.roll(z, ext - 1, 0)[HALO:HALO + r] * cw_ref[2:3, off:off + FF_TILE]
                  + cb_ref[:, off:off + FF_TILE])
            parts.append(zc)
        val, gate = parts
        t = jnp.tanh(gate * (GELU_C0 + (GELU_C0 * GELU_C1) * (gate * gate)))
        hid_scr[:, jj * FF_TILE:(jj + 1) * FF_TILE] = ((0.5 * gate * val) * (1.0 + t)).astype(BF16)
    x1 = x_ref[...] + mix_ref[...].astype(F32)
    x2 = x1 + jnp.dot(hid_scr[...], w_down_ref[...], preferred_element_type=F32)
    o_ref[...] = _rms(x2, nfin_ref[...])


def _conv_ffn(x, mix, h, w_up, conv_w, conv_b, w_down, norm_final, seq):
    t = x.shape[0]
    r = FFN_TILE
    assert seq % r == 0, "token tiles must not straddle sequences"
    tiles_per_seq = seq // r
    hb = r // HALO
    last = t // HALO - 1
    tok = lambda i: (i, 0)
    return pl.pallas_call(
        functools.partial(_conv_ffn_kernel, tiles_per_seq=tiles_per_seq),
        grid=(t // r,),
        in_specs=[pl.BlockSpec((r, D_MODEL), tok), pl.BlockSpec((r, D_MODEL), tok),
                  pl.BlockSpec((r, D_MODEL), tok),
                  pl.BlockSpec((HALO, D_MODEL), lambda i: (jnp.maximum(i * hb - 1, 0), 0)),
                  pl.BlockSpec((HALO, D_MODEL), lambda i: (jnp.minimum((i + 1) * hb, last), 0)),
                  _const_spec((D_MODEL, 2 * D_FF)), _const_spec((3, 2 * D_FF)),
                  _const_spec((1, 2 * D_FF)), _const_spec((D_FF, D_MODEL)),
                  _const_spec((1, D_MODEL))],
        out_specs=pl.BlockSpec((r, D_MODEL), tok),
        out_shape=jax.ShapeDtypeStruct((t, D_MODEL), F32),
        scratch_shapes=[pltpu.VMEM((r, D_FF), BF16)],
        compiler_params=pltpu.CompilerParams(
            dimension_semantics=("arbitrary",), vmem_limit_bytes=VMEM_LIMIT),
        name="conv_ffn",
    )(x, mix, h, h, h, w_up, conv_w, conv_b, w_down, norm_final)


def _rotary_tables(seq):
    half = HEAD_DIM // 2
    inv_freq = ROPE_BASE ** (-jnp.arange(half, dtype=F32) / half)
    ang = jnp.arange(seq, dtype=F32)[:, None] * inv_freq[None, :]
    cos = jnp.cos(ang)
    sin = jnp.sin(ang)
    return jnp.concatenate([cos, cos], axis=1), jnp.concatenate([-sin, sin], axis=1)


def _layer(x, prep):
    b, seq, _ = x.shape
    x2 = x.reshape(b * seq, D_MODEL)
    q, k, v, g, u = _in_proj(x2, prep["norm_mix"], prep["w_in"], prep["cos"], prep["sin"],
                             prep["gn_gain"], seq)
    ret = _retention(q, k, v, g, prep["ret_tabs"], seq)
    ssm = _s5(u, prep["s5"], seq)
    mix, h = _mix_norm(x2, ret, ssm, prep["glu_w"], prep["glu_b"], prep["w_out"], prep["norm_ffn"])
    out = _conv_ffn(x2, mix, h, prep["w_up"], prep["conv_w"], prep["conv_b"], prep["w_down"],
                    prep["norm_final"], seq)
    return out.reshape(b, seq, D_MODEL)


def kernel(x_prompt, x_sample, norm_mix, w_in, ret_gn_gain, s5_lambda_re, s5_lambda_im, s5_log_dt, s5_B_re, s5_B_im, s5_C_re, s5_C_im, s5_D, s5_glu_w, s5_glu_b, w_out, norm_ffn, w_up, conv_w, conv_b, w_down, norm_final):
    assert norm_mix.shape[0] == 1, "single-layer trunk"
    seq = x_prompt.shape[1]
    assert x_sample.shape[1] == seq
    cos, sin = _rotary_tables(seq)
    prep = {
        "norm_mix": norm_mix[0][None].astype(F32),
        "w_in": w_in[0].astype(BF16),
        "cos": cos, "sin": sin,
        "ret_tabs": _retention_tables(),
        "gn_gain": ret_gn_gain[0][None].astype(F32),
        "s5": _s5_tables(s5_lambda_re[0], s5_lambda_im[0], s5_log_dt[0], s5_B_re[0], s5_B_im[0],
                         s5_C_re[0], s5_C_im[0], s5_D[0]),
        "glu_w": s5_glu_w[0].astype(BF16),
        "glu_b": s5_glu_b[0][None].astype(F32),
        "w_out": w_out[0].astype(BF16),
        "norm_ffn": norm_ffn[0][None].astype(F32),
        "w_up": w_up[0].astype(BF16),
        "conv_w": conv_w[0].astype(F32),
        "conv_b": conv_b[0][None].astype(F32),
        "w_down": w_down[0].astype(BF16),
        "norm_final": norm_final[None].astype(F32),
    }
    return (_layer(x_prompt, prep), _layer(x_sample, prep))
```

```python
import functools

import jax
import jax.numpy as jnp
from jax import lax
from jax.experimental import pallas as pl
from jax.experimental.pallas import tpu as pltpu

D_MODEL = 1024
RET_WIDTH = 512
SSM_WIDTH = 512
RET_HEADS = 4
HEAD_DIM = 128
ROPE_BASE = 10000.0
SSM_GROUP = 16
SSM_GROUPS = 32
SSM_STATE = 64
D_FF = 2816
EPS = 1e-6
GELU_C0 = 0.7978845608028654
GELU_C1 = 0.044715
IN_WIDTH = 4 * RET_WIDTH + SSM_WIDTH

LANES = 128
SUBLANES = 8
BF16_ROWS = 16
VMEM_LIMIT = 56 * 1024 * 1024

RET_BLOCK = 256
SSM_BLOCKS = SSM_WIDTH // LANES
GROUPS_PER_BLOCK = LANES // SSM_GROUP
SSM_CHUNK = 16
SSM_SEQS = 4
TOK_TILE = 1024
FFN_TILE = 1024
MIX_TILE = 1024
FF_TILE = 256
HALO = BF16_ROWS

BF16 = jnp.bfloat16
F32 = jnp.float32


def _rms(x, w):
    return x * lax.rsqrt(jnp.mean(x * x, axis=-1, keepdims=True) + EPS) * w


def _const_spec(shape):
    nd = len(shape)
    return pl.BlockSpec(shape, lambda *_: (0,) * nd, pipeline_mode=pl.Buffered(1))


def _in_proj_kernel(x_ref, nw_ref, w_ref, cos_ref, sin_ref, gain_ref,
                    q_ref, k_ref, v_ref, g_ref, u_ref):
    h = _rms(x_ref[...], nw_ref[...]).astype(BF16)
    proj = jnp.dot(h, w_ref[...], preferred_element_type=F32)
    cos = cos_ref[...]
    sin = sin_ref[...]
    kscale = HEAD_DIM ** -0.5
    for hd in range(RET_HEADS):
        lo = hd * HEAD_DIM
        qh = proj[:, lo:lo + HEAD_DIM]
        q_ref[:, lo:lo + HEAD_DIM] = (qh * cos + pltpu.roll(qh, HEAD_DIM // 2, 1) * sin).astype(BF16)
        kh = proj[:, RET_WIDTH + lo:RET_WIDTH + lo + HEAD_DIM]
        kr = kh * cos + pltpu.roll(kh, HEAD_DIM // 2, 1) * sin
        k_ref[:, lo:lo + HEAD_DIM] = (kr * kscale).astype(BF16)
    v_ref[...] = proj[:, 2 * RET_WIDTH:3 * RET_WIDTH].astype(BF16)
    g_ref[...] = (jax.nn.silu(proj[:, 3 * RET_WIDTH:4 * RET_WIDTH]) * gain_ref[...]).astype(BF16)
    for blk in range(SSM_BLOCKS):
        lo = 4 * RET_WIDTH + blk * LANES
        u_ref[blk] = proj[:, lo:lo + LANES]


def _in_proj(x2, norm_w, w_in, cos2, sin2, gain, seq):
    t = x2.shape[0]
    r = TOK_TILE
    assert seq % r == 0, "token tiles must not straddle sequences"
    tiles_per_seq = seq // r
    tok = lambda i: (i, 0)
    out_tok = jax.ShapeDtypeStruct((t, RET_WIDTH), BF16)
    return pl.pallas_call(
        _in_proj_kernel,
        grid=(t // r,),
        in_specs=[
            pl.BlockSpec((r, D_MODEL), tok),
            _const_spec((1, D_MODEL)),
            _const_spec((D_MODEL, IN_WIDTH)),
            pl.BlockSpec((r, HEAD_DIM), lambda i: (i % tiles_per_seq, 0)),
            pl.BlockSpec((r, HEAD_DIM), lambda i: (i % tiles_per_seq, 0)),
            _const_spec((1, RET_WIDTH)),
        ],
        out_specs=[
            pl.BlockSpec((r, RET_WIDTH), tok),
            pl.BlockSpec((r, RET_WIDTH), tok),
            pl.BlockSpec((r, RET_WIDTH), tok),
            pl.BlockSpec((r, RET_WIDTH), tok),
            pl.BlockSpec((SSM_BLOCKS, r, LANES), lambda i: (0, i, 0)),
        ],
        out_shape=[out_tok, out_tok, out_tok, out_tok,
                   jax.ShapeDtypeStruct((SSM_BLOCKS, t, LANES), F32)],
        compiler_params=pltpu.CompilerParams(
            dimension_semantics=("arbitrary",), vmem_limit_bytes=VMEM_LIMIT),
        name="in_proj",
    )(x2, norm_w, w_in, cos2, sin2, gain)


def _retention_kernel(q_ref, k_ref, v_ref, g_ref, dmat_ref, cd_ref, row_ref, o_ref,
                      kv_scr, r_scr):
    c = RET_BLOCK
    nchunks = q_ref.shape[0] // c
    heads = [slice(hd * HEAD_DIM, (hd + 1) * HEAD_DIM) for hd in range(RET_HEADS)]

    for n in range(nchunks):
        rows = slice(n * c, (n + 1) * c)
        for hd, cols in enumerate(heads):
            kn = k_ref[rows, cols]
            kcat = jnp.concatenate([kn * row_ref[hd, 0], kn * row_ref[hd, 1]], axis=1)
            kv_scr[n, hd] = lax.dot_general(kcat, v_ref[rows, cols], (((0,), (0,)), ((), ())),
                                            preferred_element_type=F32)

    for hd in range(RET_HEADS):
        cd = cd_ref[hd]
        rf = jnp.zeros((HEAD_DIM, HEAD_DIM), F32)
        for n in range(nchunks):
            r_scr[n, hd, 0:HEAD_DIM, :] = rf.astype(BF16)
            rf = cd * rf + kv_scr[n, hd, 0:HEAD_DIM, :]
        rb = jnp.zeros((HEAD_DIM, HEAD_DIM), F32)
        for n in reversed(range(nchunks)):
            r_scr[n, hd, HEAD_DIM:2 * HEAD_DIM, :] = rb.astype(BF16)
            rb = cd * rb + kv_scr[n, hd, HEAD_DIM:2 * HEAD_DIM, :]

    for n in range(nchunks):
        rows = slice(n * c, (n + 1) * c)
        for hd, cols in enumerate(heads):
            qn = q_ref[rows, cols]
            s = lax.dot_general(qn, k_ref[rows, cols], (((1,), (1,)), ((), ())),
                                preferred_element_type=F32) * dmat_ref[hd]
            inner = jnp.dot(s.astype(BF16), v_ref[rows, cols], preferred_element_type=F32)
            qcat = jnp.concatenate([qn * row_ref[hd, 2], qn * row_ref[hd, 3]], axis=1)
            o = inner + jnp.dot(qcat, r_scr[n, hd], preferred_element_type=F32)
            o = o * lax.rsqrt(jnp.mean(o * o, axis=-1, keepdims=True) + EPS)
            o_ref[rows, cols] = (g_ref[rows, cols].astype(F32) * o).astype(BF16)


def _retention(q, k, v, g, tabs, seq):
    dmat, cd, tab_rows = tabs
    t = q.shape[0]
    nchunks = seq // RET_BLOCK
    blk = pl.BlockSpec((seq, RET_WIDTH), lambda b: (b, 0))
    return pl.pallas_call(
        _retention_kernel,
        grid=(t // seq,),
        in_specs=[blk, blk, blk, blk,
                  _const_spec((RET_HEADS, RET_BLOCK, RET_BLOCK)),
                  _const_spec((RET_HEADS, HEAD_DIM, HEAD_DIM)),
                  _const_spec((RET_HEADS, 4, RET_BLOCK, HEAD_DIM))],
        out_specs=blk,
        out_shape=jax.ShapeDtypeStruct((t, RET_WIDTH), BF16),
        scratch_shapes=[pltpu.VMEM((nchunks, RET_HEADS, 2 * HEAD_DIM, HEAD_DIM), F32),
                        pltpu.VMEM((nchunks, RET_HEADS, 2 * HEAD_DIM, HEAD_DIM), BF16)],
        compiler_params=pltpu.CompilerParams(
            dimension_semantics=("arbitrary",), vmem_limit_bytes=VMEM_LIMIT),
        name="retention",
    )(q, k, v, g, dmat, cd, tab_rows)


def _retention_tables():
    c = RET_BLOCK
    lg = jnp.log(1.0 - 2.0 ** (-5.0 - jnp.arange(RET_HEADS, dtype=F32)))[:, None, None]
    idx = jnp.arange(c, dtype=F32)
    col = jnp.broadcast_to(idx[:, None], (c, HEAD_DIM))[None]
    dmat = jnp.exp(lg * jnp.abs(idx[:, None] - idx[None, :])[None])
    cd = jnp.exp(lg * c) * jnp.ones((1, HEAD_DIM, HEAD_DIM), F32)
    zf = jnp.exp(lg * (c - 1.0 - col))
    zb = jnp.exp(lg * col)
    xf = jnp.exp(lg * (col + 1.0))
    xb = jnp.exp(lg * (c - col))
    return dmat, cd, jnp.stack([zf, zb, xf, xb], axis=1).astype(BF16)


def _cmul(ar, ai, br, bi):
    return ar * br - ai * bi, ar * bi + ai * br


def _roll_lanes(x, shift):
    ntiles = x.shape[1] // LANES
    whole, part = divmod(shift % x.shape[1], LANES)
    tiles = [x[:, k * LANES:(k + 1) * LANES] for k in range(ntiles)]
    tiles = [tiles[(k - whole) % ntiles] for k in range(ntiles)]
    if part:
        lane = lax.broadcasted_iota(jnp.int32, tiles[0].shape, 1)
        rolled = [pltpu.roll(t, part, 1) for t in tiles]
        tiles = [jnp.where(lane < part, rolled[(k - 1) % ntiles], rolled[k]) for k in range(ntiles)]
    return tiles[0] if ntiles == 1 else jnp.concatenate(tiles, axis=1)


def _split_dot(a, z):
    dims = (((1,), (1,)), ((), ()))
    a_hi = a.astype(BF16)
    z_hi = z.astype(BF16)
    a_lo = (a - a_hi.astype(F32)).astype(BF16)
    z_lo = (z - z_hi.astype(F32)).astype(BF16)
    return (lax.dot_general(a_hi, z_hi, dims, preferred_element_type=F32)
            + lax.dot_general(a_hi, z_lo, dims, preferred_element_type=F32)
            + lax.dot_general(a_lo, z_hi, dims, preferred_element_type=F32))


def _s5_kernel(u_ref, perm_ref, bt_ref, ct_ref, pw_ref, dt0_ref, tab_ref, y_ref,
               m_scr, w_scr, vt_scr, g_scr, u_scr, s_scr, x_scr, *, nseq, seq):
    c = SSM_CHUNK
    sub = c // SUBLANES
    gpb = GROUPS_PER_BLOCK
    npair = gpb // 2
    width = c * SSM_GROUP
    sp = npair * LANES
    rows8 = nseq * seq // SUBLANES
    rows = rows8 // sub
    nrows = seq // c

    @pl.when(pl.program_id(1) == 0)
    def _expand():
        slot = lax.broadcasted_iota(jnp.int32, (SSM_GROUP, width), 1) >> 4
        for g in range(gpb):
            pair, gi = divmod(g, 2)
            zt = ([], [])
            for d in range(2):
                b_re, b_im = bt_ref[d, g, 0], bt_ref[d, g, 1]
                c_re, c_im = ct_ref[d, g, 0], ct_ref[d, g, 1]
                for s in range(c):
                    r16 = slice(gi * width + s * SSM_GROUP, gi * width + (s + 1) * SSM_GROUP)
                    re_cols = slice((2 * d) * LANES, (2 * d + 1) * LANES)
                    im_cols = slice((2 * d + 1) * LANES, (2 * d + 2) * LANES)
                    p_w = c - 1 - s if d == 0 else s
                    p_v = s + 1 if d == 0 else c - s
                    p_z = s if d == 0 else c - 1 - s
                    wr, wi = _cmul(b_re, b_im, pw_ref[p_w, d, pair, 0], pw_ref[p_w, d, pair, 1])
                    w_scr[pair, r16, re_cols] = wr.astype(BF16)
                    w_scr[pair, r16, im_cols] = wi.astype(BF16)
                    er, ei = _cmul(c_re, c_im, pw_ref[p_v, d, pair, 0], pw_ref[p_v, d, pair, 1])
                    vt_scr[pair, r16, re_cols] = er.astype(BF16)
                    vt_scr[pair, r16, im_cols] = (-ei).astype(BF16)
                    er, ei = _cmul(c_re, c_im, pw_ref[p_z, d, pair, 0], pw_ref[p_z, d, pair, 1])
                    zt[d].append(jnp.concatenate([er, -ei], axis=1))
            bcat = [jnp.concatenate([bt_ref[d, g, 0], bt_ref[d, g, 1]], axis=1) for d in range(2)]
            f0 = _split_dot(bcat[0], jnp.concatenate(zt[0], axis=0)) + dt0_ref[g]
            b0 = _split_dot(bcat[1], jnp.concatenate(zt[1], axis=0))
            for s in range(c):
                fwd = jnp.where(slot >= s, _roll_lanes(f0, SSM_GROUP * s), 0.0)
                bwd = jnp.where(slot <= s, _roll_lanes(b0, SSM_GROUP * (s + 1)), 0.0)
                m_scr[g, s * SSM_GROUP:(s + 1) * SSM_GROUP, :] = (fwd + bwd).astype(BF16)

    ucat = jnp.concatenate(
        [u_ref[pl.ds(t, rows8, stride=SUBLANES), :].astype(BF16) for t in range(SUBLANES)], axis=1)
    grouped = jnp.dot(ucat, perm_ref[...], preferred_element_type=F32)
    for g in range(gpb):
        g_scr[g] = grouped[:, g * LANES:(g + 1) * LANES]
    for g in range(gpb):
        u_scr[g] = jnp.concatenate(
            [g_scr[g, pl.ds(k, rows, stride=sub), :] for k in range(sub)], axis=1).astype(BF16)

    for pair in range(npair):
        upair = jnp.concatenate([u_scr[2 * pair], u_scr[2 * pair + 1]], axis=1)
        sg = jnp.dot(upair, w_scr[pair], preferred_element_type=F32)
        for k in range(4):
            s_scr[:, k * sp + pair * LANES:k * sp + (pair + 1) * LANES] = sg[:, k * LANES:(k + 1) * LANES]

    nblk = nrows // SUBLANES
    row = lax.broadcasted_iota(jnp.int32, (SUBLANES, LANES), 0)

    def scan_block(pair, r0, d, carry):
        rows_ = slice(r0, r0 + SUBLANES)
        lanes_ = slice(pair * LANES, (pair + 1) * LANES)
        re_cols = slice(2 * d * sp + pair * LANES, 2 * d * sp + (pair + 1) * LANES)
        im_cols = slice((2 * d + 1) * sp + pair * LANES, (2 * d + 1) * sp + (pair + 1) * LANES)
        pr = s_scr[rows_, re_cols]
        pi = s_scr[rows_, im_cols]
        for step, sh in enumerate((1, 2, 4)):
            shift = sh if d == 0 else SUBLANES - sh
            tr, ti = _cmul(tab_ref[d, step, 0, :, lanes_], tab_ref[d, step, 1, :, lanes_],
                           pltpu.roll(pr, shift, 0), pltpu.roll(pi, shift, 0))
            pr = pr + tr
            pi = pi + ti
        cr, ci = carry
        tr, ti = _cmul(tab_ref[d, 3, 0, :, lanes_], tab_ref[d, 3, 1, :, lanes_], cr, ci)
        xr = pr + tr
        xi = pi + ti
        edge = 0 if d == 0 else SUBLANES - 1
        shift = 1 if d == 0 else SUBLANES - 1
        x_scr[rows_, re_cols] = jnp.where(row == edge, cr, pltpu.roll(xr, shift, 0))
        x_scr[rows_, im_cols] = jnp.where(row == edge, ci, pltpu.roll(xi, shift, 0))
        last = SUBLANES - 1 if d == 0 else 0
        return (jnp.broadcast_to(xr[last:last + 1, :], (SUBLANES, LANES)),
                jnp.broadcast_to(xi[last:last + 1, :], (SUBLANES, LANES)))

    zero = jnp.zeros((SUBLANES, LANES), F32)

    for pair in range(npair):
        for sq in range(nseq):
            cf = cb = (zero, zero)
            for i in range(nblk):
                cf = scan_block(pair, sq * nrows + i * SUBLANES, 0, cf)
                cb = scan_block(pair, sq * nrows + (nblk - 1 - i) * SUBLANES, 1, cb)
        xpair = jnp.concatenate(
            [x_scr[:, k * sp + pair * LANES:k * sp + (pair + 1) * LANES] for k in range(4)],
            axis=1).astype(BF16)
        cross = lax.dot_general(xpair, vt_scr[pair], (((1,), (1,)), ((), ())),
                                preferred_element_type=F32)
        for gi in range(2):
            g = 2 * pair + gi
            yg = (jnp.dot(u_scr[g], m_scr[g], preferred_element_type=F32)
                  + cross[:, gi * width:(gi + 1) * width])
            for k in range(sub):
                g_scr[g, pl.ds(k, rows, stride=sub), :] = yg[:, k * LANES:(k + 1) * LANES]

    ycat = jnp.concatenate([g_scr[g] for g in range(gpb)], axis=1).astype(BF16)
    ytok = jnp.dot(ycat, perm_ref[...], preferred_element_type=F32)
    for t in range(SUBLANES):
        y_ref[t] = ytok[:, t * LANES:(t + 1) * LANES].astype(BF16)


def _s5(u_blocks, mats, seq):
    perm, bt, ct, pw, dt0, tabs = mats
    nb, t, _ = u_blocks.shape
    t8 = t // SUBLANES
    c = SSM_CHUNK
    gpb = GROUPS_PER_BLOCK
    npair = gpb // 2
    nseq = SSM_SEQS
    while (t // seq) % nseq:
        nseq //= 2
    rows8 = nseq * seq // SUBLANES
    rows = nseq * seq // c
    width = c * SSM_GROUP
    sp = npair * LANES
    return pl.pallas_call(
        functools.partial(_s5_kernel, nseq=nseq, seq=seq),
        grid=(nb, t // (nseq * seq)),
        in_specs=[
            pl.BlockSpec((None, nseq * seq, LANES), lambda b, i: (b, i, 0)),
            _const_spec((SUBLANES * LANES, SUBLANES * LANES)),
            pl.BlockSpec((None, 2, gpb, 2, SSM_GROUP, LANES), lambda b, i: (b, 0, 0, 0, 0, 0)),
            pl.BlockSpec((None, 2, gpb, 2, SSM_GROUP, LANES), lambda b, i: (b, 0, 0, 0, 0, 0)),
            pl.BlockSpec((None, c + 1, 2, npair, 2, 1, LANES),
                         lambda b, i: (b, 0, 0, 0, 0, 0, 0)),
            pl.BlockSpec((None, gpb, SSM_GROUP, width), lambda b, i: (b, 0, 0, 0)),
            pl.BlockSpec((None, 2, 4, 2, SUBLANES, sp), lambda b, i: (b, 0, 0, 0, 0, 0)),
        ],
        out_specs=pl.BlockSpec((None, SUBLANES, rows8, LANES), lambda b, i: (b, 0, i, 0)),
        out_shape=jax.ShapeDtypeStruct((nb, SUBLANES, t8, LANES), BF16),
        scratch_shapes=[pltpu.VMEM((gpb, width, width), BF16),
                        pltpu.VMEM((npair, 2 * width, 4 * LANES), BF16),
                        pltpu.VMEM((npair, 2 * width, 4 * LANES), BF16),
                        pltpu.VMEM((gpb, rows8, LANES), F32),
                        pltpu.VMEM((gpb, rows, width), BF16),
                        pltpu.VMEM((rows, 4 * sp), F32),
                        pltpu.VMEM((rows, 4 * sp), F32)],
        compiler_params=pltpu.CompilerParams(
            dimension_semantics=("arbitrary", "arbitrary"), vmem_limit_bytes=VMEM_LIMIT),
        name="s5",
    )(u_blocks, perm, bt, ct, pw, dt0, tabs)


def _s5_tables(lam_re, lam_im, log_dt, b_re, b_im, c_re, c_im, d_skip):
    c = SSM_CHUNK
    nb, gpb = SSM_BLOCKS, GROUPS_PER_BLOCK
    npair = gpb // 2
    lr = jnp.minimum(lam_re.astype(F32), -1e-4)
    li = lam_im.astype(F32)
    dt = jnp.exp(log_dt.astype(F32))[..., None]
    plist = list(range(c + 1)) + [c * k for k in range(2, SUBLANES + 1)]
    pidx = {pw_: i for i, pw_ in enumerate(plist)}
    p = jnp.asarray(plist, F32)[:, None, None, None]
    pmag = jnp.exp(p * (lr * dt)[None])
    pr = pmag * jnp.cos(p * (li * dt)[None])
    pi = pmag * jnp.sin(p * (li * dt)[None])
    ar, ai = pr[1], pi[1]
    den = lr * lr + li * li
    nr = ar - 1.0
    ni = ai
    cr = ((nr * lr + ni * li) / den)[:, :, None, :]
    ci = ((ni * lr - nr * li) / den)[:, :, None, :]
    b_re_t = b_re.astype(F32).transpose(0, 1, 3, 2)
    b_im_t = b_im.astype(F32).transpose(0, 1, 3, 2)
    bbar = jnp.stack([cr * b_re_t - ci * b_im_t, cr * b_im_t + ci * b_re_t], axis=2)
    cmat = jnp.stack([c_re.astype(F32), c_im.astype(F32)], axis=2)

    def half_placed(x):
        own = (jnp.arange(SSM_GROUPS) % 2)[:, None] == jnp.arange(2)[None, :]
        wide = jnp.where(own[None, :, None, None, :, None], x[:, :, :, :, None, :], 0.0)
        wide = wide.reshape(2, nb, gpb, 2, SSM_GROUP, LANES)
        return jnp.moveaxis(wide, 1, 0)

    pows = jnp.stack([pr, pi], axis=3)[:c + 1]
    pw = pows.reshape(c + 1, 2, nb, npair, 2, 2, SSM_STATE).transpose(2, 0, 1, 3, 5, 4, 6)
    pw = pw.reshape(nb, c + 1, 2, npair, 2, 1, LANES)

    lane = jnp.arange(c * SSM_GROUP)
    dt0 = jnp.where(lane[None, None, :] == jnp.arange(SSM_GROUP)[None, :, None],
                    d_skip.astype(F32).reshape(SSM_GROUPS, SSM_GROUP, 1), 0.0)
    dt0 = dt0.reshape(nb, gpb, SSM_GROUP, c * SSM_GROUP)

    rowi = jnp.arange(SUBLANES)

    def lanes(x):
        return x.reshape(x.shape[:-2] + (nb, npair * LANES))

    tabs = []
    for d in range(2):
        per_dir = []
        for sh in (1, 2, 4):
            keep = (rowi >= sh) if d == 0 else (rowi < SUBLANES - sh)
            per_dir.append(jnp.stack([
                jnp.where(keep[:, None, None], lanes(pr[pidx[sh * c], d])[None], 0.0),
                jnp.where(keep[:, None, None], lanes(pi[pidx[sh * c], d])[None], 0.0)]))
        mult = range(1, SUBLANES + 1) if d == 0 else range(SUBLANES, 0, -1)
        cpow = jnp.asarray([pidx[k * c] for k in mult])
        per_dir.append(jnp.stack([lanes(pr[cpow, d]), lanes(pi[cpow, d])]))
        tabs.append(jnp.stack(per_dir))
    tabs = jnp.stack(tabs).transpose(4, 0, 1, 2, 3, 5)

    src = jnp.arange(SUBLANES * LANES)
    dst = ((src >> 4) & 7) * LANES + (src >> 7) * SSM_GROUP + (src & 15)
    perm = (dst[:, None] == src[None, :]).astype(BF16)
    return perm, half_placed(bbar), half_placed(cmat), pw, dt0, tabs


def _mix_kernel(ret_ref, ssm_ref, glu_w_ref, glu_b_ref, w_out_ref, mix_ref, tok_scr):
    rows8 = ret_ref.shape[0] // SUBLANES
    for b in range(SSM_BLOCKS):
        for t in range(SUBLANES):
            tok_scr[b, pl.ds(t, rows8, stride=SUBLANES), :] = ssm_ref[b, t].astype(F32)
    ssm = jnp.concatenate([tok_scr[b] for b in range(SSM_BLOCKS)], axis=1)
    y = jax.nn.gelu(ssm)
    gate = jax.nn.sigmoid(
        jnp.dot(y.astype(BF16), glu_w_ref[...], preferred_element_type=F32) + glu_b_ref[...])
    cat = jnp.concatenate([ret_ref[...], (y * gate).astype(BF16)], axis=1)
    mix_ref[...] = jnp.dot(cat, w_out_ref[...], preferred_element_type=F32).astype(BF16)


def _mix(ret, ssm, glu_w, glu_b, w_out):
    t = ret.shape[0]
    r = MIX_TILE
    tok = lambda i: (i, 0)
    return pl.pallas_call(
        _mix_kernel,
        grid=(t // r,),
        in_specs=[pl.BlockSpec((r, RET_WIDTH), tok),
                  pl.BlockSpec((SSM_BLOCKS, SUBLANES, r // SUBLANES, LANES),
                               lambda i: (0, 0, i, 0)),
                  _const_spec((SSM_WIDTH, SSM_WIDTH)), _const_spec((1, SSM_WIDTH)),
                  _const_spec((D_MODEL, D_MODEL))],
        out_specs=pl.BlockSpec((r, D_MODEL), tok),
        out_shape=jax.ShapeDtypeStruct((t, D_MODEL), BF16),
        scratch_shapes=[pltpu.VMEM((SSM_BLOCKS, r, LANES), F32)],
        compiler_params=pltpu.CompilerParams(
            dimension_semantics=("arbitrary",), vmem_limit_bytes=VMEM_LIMIT),
        name="mix",
    )(ret, ssm, glu_w, glu_b, w_out)


def _conv_ffn_kernel(xm_ref, xp_ref, xn_ref, mm_ref, mp_ref, mn_ref, nffn_ref, w_up_ref, cw_ref,
                     cb_ref, w_down_ref, nfin_ref, o_ref, hid_scr, *, tiles_per_seq):
    r = xm_ref.shape[0]
    ext = r + 2 * HALO
    nffn = nffn_ref[...]
    j = pl.program_id(0) % tiles_per_seq
    hp = _rms(xp_ref[...] + mp_ref[...].astype(F32), nffn).astype(BF16)
    hn = _rms(xn_ref[...] + mn_ref[...].astype(F32), nffn).astype(BF16)
    hp = jnp.where(j == 0, jnp.zeros(hp.shape, BF16), hp)
    hn = jnp.where(j == tiles_per_seq - 1, jnp.zeros(hn.shape, BF16), hn)
    hm = _rms(xm_ref[...] + mm_ref[...].astype(F32), nffn).astype(BF16)
    h = jnp.concatenate([hp, hm, hn], axis=0)
    for jj in range(D_FF // FF_TILE):
        parts = []
        for off in (jj * FF_TILE, D_FF + jj * FF_TILE):
            z = jnp.dot(h, w_up_ref[:, off:off + FF_TILE], preferred_element_type=F32)
            zc = (pltpu.roll(z, 1, 0)[HALO:HALO + r] * cw_ref[0:1, off:off + FF_TILE]
                  + z[HALO:HALO + r] * cw_ref[1:2, off:off + FF_TILE]
                  + pltpu.roll(z, ext - 1, 0)[HALO:HALO + r] * cw_ref[2:3, off:off + FF_TILE]
                  + cb_ref[:, off:off + FF_TILE])
            parts.append(zc)
        val, gate = parts
        t = jnp.tanh(gate * (GELU_C0 + (GELU_C0 * GELU_C1) * (gate * gate)))
        hid_scr[:, jj * FF_TILE:(jj + 1) * FF_TILE] = ((0.5 * gate * val) * (1.0 + t)).astype(BF16)
    x1 = xm_ref[...] + mm_ref[...].astype(F32)
    x2 = x1 + jnp.dot(hid_scr[...], w_down_ref[...], preferred_element_type=F32)
    o_ref[...] = _rms(x2, nfin_ref[...])


def _conv_ffn(x, mix, norm_ffn, w_up, conv_w, conv_b, w_down, norm_final, seq):
    t = x.shape[0]
    r = FFN_TILE
    assert seq % r == 0, "token tiles must not straddle sequences"
    tiles_per_seq = seq // r
    hb = r // HALO
    last = t // HALO - 1
    trio = [pl.BlockSpec((r, D_MODEL), lambda i: (i, 0)),
            pl.BlockSpec((HALO, D_MODEL), lambda i: (jnp.maximum(i * hb - 1, 0), 0)),
            pl.BlockSpec((HALO, D_MODEL), lambda i: (jnp.minimum((i + 1) * hb, last), 0))]
    tok = lambda i: (i, 0)
    return pl.pallas_call(
        functools.partial(_conv_ffn_kernel, tiles_per_seq=tiles_per_seq),
        grid=(t // r,),
        in_specs=trio + trio + [
                  _const_spec((1, D_MODEL)),
                  _const_spec((D_MODEL, 2 * D_FF)), _const_spec((3, 2 * D_FF)),
                  _const_spec((1, 2 * D_FF)), _const_spec((D_FF, D_MODEL)),
                  _const_spec((1, D_MODEL))],
        out_specs=pl.BlockSpec((r, D_MODEL), tok),
        out_shape=jax.ShapeDtypeStruct((t, D_MODEL), F32),
        scratch_shapes=[pltpu.VMEM((r, D_FF), BF16)],
        compiler_params=pltpu.CompilerParams(
            dimension_semantics=("arbitrary",), vmem_limit_bytes=VMEM_LIMIT),
        name="conv_ffn",
    )(x, x, x, mix, mix, mix, norm_ffn, w_up, conv_w, conv_b, w_down, norm_final)


def _rotary_tables(seq):
    half = HEAD_DIM // 2
    inv_freq = ROPE_BASE ** (-jnp.arange(half, dtype=F32) / half)
    ang = jnp.arange(seq, dtype=F32)[:, None] * inv_freq[None, :]
    cos = jnp.cos(ang)
    sin = jnp.sin(ang)
    return jnp.concatenate([cos, cos], axis=1), jnp.concatenate([-sin, sin], axis=1)


def _layer(x, prep):
    b, seq, _ = x.shape
    x2 = x.reshape(b * seq, D_MODEL)
    q, k, v, g, u = _in_proj(x2, prep["norm_mix"], prep["w_in"], prep["cos"], prep["sin"],
                             prep["gn_gain"], seq)
    ret = _retention(q, k, v, g, prep["ret_tabs"], seq)
    ssm = _s5(u, prep["s5"], seq)
    mix = _mix(ret, ssm, prep["glu_w"], prep["glu_b"], prep["w_out"])
    out = _conv_ffn(x2, mix, prep["norm_ffn"], prep["w_up"], prep["conv_w"], prep["conv_b"],
                    prep["w_down"], prep["norm_final"], seq)
    return out.reshape(b, seq, D_MODEL)


def kernel(x_prompt, x_sample, norm_mix, w_in, ret_gn_gain, s5_lambda_re, s5_lambda_im, s5_log_dt, s5_B_re, s5_B_im, s5_C_re, s5_C_im, s5_D, s5_glu_w, s5_glu_b, w_out, norm_ffn, w_up, conv_w, conv_b, w_down, norm_final):
    assert norm_mix.shape[0] == 1, "single-layer trunk"
    seq = x_prompt.shape[1]
    assert x_sample.shape[1] == seq
    cos, sin = _rotary_tables(seq)
    prep = {
        "norm_mix": norm_mix[0][None].astype(F32),
        "w_in": w_in[0].astype(BF16),
        "cos": cos, "sin": sin,
        "ret_tabs": _retention_tables(),
        "gn_gain": ret_gn_gain[0][None].astype(F32),
        "s5": _s5_tables(s5_lambda_re[0], s5_lambda_im[0], s5_log_dt[0], s5_B_re[0], s5_B_im[0],
                         s5_C_re[0], s5_C_im[0], s5_D[0]),
        "glu_w": s5_glu_w[0].astype(BF16),
        "glu_b": s5_glu_b[0][None].astype(F32),
        "w_out": w_out[0].astype(BF16),
        "norm_ffn": norm_ffn[0][None].astype(F32),
        "w_up": w_up[0].astype(BF16),
        "conv_w": conv_w[0].astype(F32),
        "conv_b": conv_b[0][None].astype(F32),
        "w_down": w_down[0].astype(BF16),
        "norm_final": norm_final[None].astype(F32),
    }
    return (_layer(x_prompt, prep), _layer(x_sample, prep))
```

```python
import functools

import jax
import jax.numpy as jnp
from jax import lax
from jax.experimental import pallas as pl
from jax.experimental.pallas import tpu as pltpu

D_MODEL = 1024
RET_WIDTH = 512
SSM_WIDTH = 512
RET_HEADS = 4
HEAD_DIM = 128
ROPE_BASE = 10000.0
SSM_GROUP = 16
SSM_GROUPS = 32
SSM_STATE = 64
D_FF = 2816
EPS = 1e-6
GELU_C0 = 0.7978845608028654
GELU_C1 = 0.044715
IN_WIDTH = 4 * RET_WIDTH + SSM_WIDTH

LANES = 128
SUBLANES = 8
BF16_ROWS = 16
VMEM_LIMIT = 56 * 1024 * 1024

RET_BLOCK = 256
SSM_BLOCKS = SSM_WIDTH // LANES
GROUPS_PER_BLOCK = LANES // SSM_GROUP
SSM_CHUNK = 16
SSM_SEQS = 4
TOK_TILE = 1024
FFN_TILE = 1024
MIX_TILE = 1024
FF_TILE = 256
HALO = BF16_ROWS

BF16 = jnp.bfloat16
F32 = jnp.float32


def _rms(x, w):
    return x * lax.rsqrt(jnp.mean(x * x, axis=-1, keepdims=True) + EPS) * w


def _const_spec(shape):
    nd = len(shape)
    return pl.BlockSpec(shape, lambda *_: (0,) * nd, pipeline_mode=pl.Buffered(1))


def _in_proj_kernel(x_ref, nw_ref, w_ref, cos_ref, sin_ref, gain_ref,
                    q_ref, k_ref, v_ref, g_ref, u_ref):
    h = _rms(x_ref[...], nw_ref[...]).astype(BF16)
    proj = jnp.dot(h, w_ref[...], preferred_element_type=F32)
    cos = cos_ref[...]
    sin = sin_ref[...]
    kscale = HEAD_DIM ** -0.5
    for hd in range(RET_HEADS):
        lo = hd * HEAD_DIM
        qh = proj[:, lo:lo + HEAD_DIM]
        q_ref[:, lo:lo + HEAD_DIM] = (qh * cos + pltpu.roll(qh, HEAD_DIM // 2, 1) * sin).astype(BF16)
        kh = proj[:, RET_WIDTH + lo:RET_WIDTH + lo + HEAD_DIM]
        kr = kh * cos + pltpu.roll(kh, HEAD_DIM // 2, 1) * sin
        k_ref[:, lo:lo + HEAD_DIM] = (kr * kscale).astype(BF16)
    v_ref[...] = proj[:, 2 * RET_WIDTH:3 * RET_WIDTH].astype(BF16)
    g_ref[...] = (jax.nn.silu(proj[:, 3 * RET_WIDTH:4 * RET_WIDTH]) * gain_ref[...]).astype(BF16)
    for blk in range(SSM_BLOCKS):
        lo = 4 * RET_WIDTH + blk * LANES
        u_ref[blk] = proj[:, lo:lo + LANES]


def _in_proj(x2, norm_w, w_in, cos2, sin2, gain, seq):
    t = x2.shape[0]
    r = TOK_TILE
    assert seq % r == 0, "token tiles must not straddle sequences"
    tiles_per_seq = seq // r
    tok = lambda i: (i, 0)
    out_tok = jax.ShapeDtypeStruct((t, RET_WIDTH), BF16)
    return pl.pallas_call(
        _in_proj_kernel,
        grid=(t // r,),
        in_specs=[
            pl.BlockSpec((r, D_MODEL), tok),
            _const_spec((1, D_MODEL)),
            _const_spec((D_MODEL, IN_WIDTH)),
            pl.BlockSpec((r, HEAD_DIM), lambda i: (i % tiles_per_seq, 0)),
            pl.BlockSpec((r, HEAD_DIM), lambda i: (i % tiles_per_seq, 0)),
            _const_spec((1, RET_WIDTH)),
        ],
        out_specs=[
            pl.BlockSpec((r, RET_WIDTH), tok),
            pl.BlockSpec((r, RET_WIDTH), tok),
            pl.BlockSpec((r, RET_WIDTH), tok),
            pl.BlockSpec((r, RET_WIDTH), tok),
            pl.BlockSpec((SSM_BLOCKS, r, LANES), lambda i: (0, i, 0)),
        ],
        out_shape=[out_tok, out_tok, out_tok, out_tok,
                   jax.ShapeDtypeStruct((SSM_BLOCKS, t, LANES), F32)],
        compiler_params=pltpu.CompilerParams(
            dimension_semantics=("arbitrary",), vmem_limit_bytes=VMEM_LIMIT),
        name="in_proj",
    )(x2, norm_w, w_in, cos2, sin2, gain)


def _retention_kernel(q_ref, k_ref, v_ref, g_ref, dmat_ref, cd_ref, row_ref, o_ref,
                      kv_scr, r_scr):
    c = RET_BLOCK
    nchunks = q_ref.shape[0] // c
    heads = [slice(hd * HEAD_DIM, (hd + 1) * HEAD_DIM) for hd in range(RET_HEADS)]

    for n in range(nchunks):
        rows = slice(n * c, (n + 1) * c)
        for hd, cols in enumerate(heads):
            kn = k_ref[rows, cols]
            kcat = jnp.concatenate([kn * row_ref[hd, 0], kn * row_ref[hd, 1]], axis=1)
            kv_scr[n, hd] = lax.dot_general(kcat, v_ref[rows, cols], (((0,), (0,)), ((), ())),
                                            preferred_element_type=F32)

    for hd in range(RET_HEADS):
        cd = cd_ref[hd]
        rf = jnp.zeros((HEAD_DIM, HEAD_DIM), F32)
        for n in range(nchunks):
            r_scr[n, hd, 0:HEAD_DIM, :] = rf.astype(BF16)
            rf = cd * rf + kv_scr[n, hd, 0:HEAD_DIM, :]
        rb = jnp.zeros((HEAD_DIM, HEAD_DIM), F32)
        for n in reversed(range(nchunks)):
            r_scr[n, hd, HEAD_DIM:2 * HEAD_DIM, :] = rb.astype(BF16)
            rb = cd * rb + kv_scr[n, hd, HEAD_DIM:2 * HEAD_DIM, :]

    for n in range(nchunks):
        rows = slice(n * c, (n + 1) * c)
        for hd, cols in enumerate(heads):
            qn = q_ref[rows, cols]
            s = lax.dot_general(qn, k_ref[rows, cols], (((1,), (1,)), ((), ())),
                                preferred_element_type=F32) * dmat_ref[hd]
            inner = jnp.dot(s.astype(BF16), v_ref[rows, cols], preferred_element_type=F32)
            qcat = jnp.concatenate([qn * row_ref[hd, 2], qn * row_ref[hd, 3]], axis=1)
            o = inner + jnp.dot(qcat, r_scr[n, hd], preferred_element_type=F32)
            o = o * lax.rsqrt(jnp.mean(o * o, axis=-1, keepdims=True) + EPS)
            o_ref[rows, cols] = (g_ref[rows, cols].astype(F32) * o).astype(BF16)


def _retention(q, k, v, g, tabs, seq):
    dmat, cd, tab_rows = tabs
    t = q.shape[0]
    nchunks = seq // RET_BLOCK
    blk = pl.BlockSpec((seq, RET_WIDTH), lambda b: (b, 0))
    return pl.pallas_call(
        _retention_kernel,
        grid=(t // seq,),
        in_specs=[blk, blk, blk, blk,
                  _const_spec((RET_HEADS, RET_BLOCK, RET_BLOCK)),
                  _const_spec((RET_HEADS, HEAD_DIM, HEAD_DIM)),
                  _const_spec((RET_HEADS, 4, RET_BLOCK, HEAD_DIM))],
        out_specs=blk,
        out_shape=jax.ShapeDtypeStruct((t, RET_WIDTH), BF16),
        scratch_shapes=[pltpu.VMEM((nchunks, RET_HEADS, 2 * HEAD_DIM, HEAD_DIM), F32),
                        pltpu.VMEM((nchunks, RET_HEADS, 2 * HEAD_DIM, HEAD_DIM), BF16)],
        compiler_params=pltpu.CompilerParams(
            dimension_semantics=("arbitrary",), vmem_limit_bytes=VMEM_LIMIT),
        name="retention",
    )(q, k, v, g, dmat, cd, tab_rows)


def _retention_tables():
    c = RET_BLOCK
    lg = jnp.log(1.0 - 2.0 ** (-5.0 - jnp.arange(RET_HEADS, dtype=F32)))[:, None, None]
    idx = jnp.arange(c, dtype=F32)
    col = jnp.broadcast_to(idx[:, None], (c, HEAD_DIM))[None]
    dmat = jnp.exp(lg * jnp.abs(idx[:, None] - idx[None, :])[None])
    cd = jnp.exp(lg * c) * jnp.ones((1, HEAD_DIM, HEAD_DIM), F32)
    zf = jnp.exp(lg * (c - 1.0 - col))
    zb = jnp.exp(lg * col)
    xf = jnp.exp(lg * (col + 1.0))
    xb = jnp.exp(lg * (c - col))
    return dmat, cd, jnp.stack([zf, zb, xf, xb], axis=1).astype(BF16)


def _cmul(ar, ai, br, bi):
    return ar * br - ai * bi, ar * bi + ai * br


def _roll_lanes(x, shift):
    ntiles = x.shape[1] // LANES
    whole, part = divmod(shift % x.shape[1], LANES)
    tiles = [x[:, k * LANES:(k + 1) * LANES] for k in range(ntiles)]
    tiles = [tiles[(k - whole) % ntiles] for k in range(ntiles)]
    if part:
        lane = lax.broadcasted_iota(jnp.int32, tiles[0].shape, 1)
        rolled = [pltpu.roll(t, part, 1) for t in tiles]
        tiles = [jnp.where(lane < part, rolled[(k - 1) % ntiles], rolled[k]) for k in range(ntiles)]
    return tiles[0] if ntiles == 1 else jnp.concatenate(tiles, axis=1)


def _split_dot(a, z):
    dims = (((1,), (1,)), ((), ()))
    a_hi = a.astype(BF16)
    z_hi = z.astype(BF16)
    a_lo = (a - a_hi.astype(F32)).astype(BF16)
    z_lo = (z - z_hi.astype(F32)).astype(BF16)
    return (lax.dot_general(a_hi, z_hi, dims, preferred_element_type=F32)
            + lax.dot_general(a_hi, z_lo, dims, preferred_element_type=F32)
            + lax.dot_general(a_lo, z_hi, dims, preferred_element_type=F32))


def _s5_kernel(u_ref, perm_ref, bt_ref, ct_ref, pw_ref, dt0_ref, tab_ref, y_ref,
               m_scr, w_scr, vt_scr, g_scr, u_scr, s_scr, x_scr, *, nseq, seq):
    c = SSM_CHUNK
    sub = c // SUBLANES
    gpb = GROUPS_PER_BLOCK
    npair = gpb // 2
    width = c * SSM_GROUP
    sp = npair * LANES
    rows8 = nseq * seq // SUBLANES
    rows = rows8 // sub
    nrows = seq // c

    @pl.when(pl.program_id(1) == 0)
    def _expand():
        slot = lax.broadcasted_iota(jnp.int32, (SSM_GROUP, width), 1) >> 4
        for g in range(gpb):
            pair, gi = divmod(g, 2)
            zt = ([], [])
            for d in range(2):
                b_re, b_im = bt_ref[d, g, 0], bt_ref[d, g, 1]
                c_re, c_im = ct_ref[d, g, 0], ct_ref[d, g, 1]
                for s in range(c):
                    r16 = slice(gi * width + s * SSM_GROUP, gi * width + (s + 1) * SSM_GROUP)
                    re_cols = slice((2 * d) * LANES, (2 * d + 1) * LANES)
                    im_cols = slice((2 * d + 1) * LANES, (2 * d + 2) * LANES)
                    p_w = c - 1 - s if d == 0 else s
                    p_v = s + 1 if d == 0 else c - s
                    p_z = s if d == 0 else c - 1 - s
                    wr, wi = _cmul(b_re, b_im, pw_ref[p_w, d, pair, 0], pw_ref[p_w, d, pair, 1])
                    w_scr[pair, r16, re_cols] = wr.astype(BF16)
                    w_scr[pair, r16, im_cols] = wi.astype(BF16)
                    er, ei = _cmul(c_re, c_im, pw_ref[p_v, d, pair, 0], pw_ref[p_v, d, pair, 1])
                    vt_scr[pair, r16, re_cols] = er.astype(BF16)
                    vt_scr[pair, r16, im_cols] = (-ei).astype(BF16)
                    er, ei = _cmul(c_re, c_im, pw_ref[p_z, d, pair, 0], pw_ref[p_z, d, pair, 1])
                    zt[d].append(jnp.concatenate([er, -ei], axis=1))
            bcat = [jnp.concatenate([bt_ref[d, g, 0], bt_ref[d, g, 1]], axis=1) for d in range(2)]
            f0 = _split_dot(bcat[0], jnp.concatenate(zt[0], axis=0)) + dt0_ref[g]
            b0 = _split_dot(bcat[1], jnp.concatenate(zt[1], axis=0))
            for s in range(c):
                fwd = jnp.where(slot >= s, _roll_lanes(f0, SSM_GROUP * s), 0.0)
                bwd = jnp.where(slot <= s, _roll_lanes(b0, SSM_GROUP * (s + 1)), 0.0)
                m_scr[g, s * SSM_GROUP:(s + 1) * SSM_GROUP, :] = (fwd + bwd).astype(BF16)

    ucat = jnp.concatenate(
        [u_ref[pl.ds(t, rows8, stride=SUBLANES), :].astype(BF16) for t in range(SUBLANES)], axis=1)
    grouped = jnp.dot(ucat, perm_ref[...], preferred_element_type=F32)
    for g in range(gpb):
        g_scr[g] = grouped[:, g * LANES:(g + 1) * LANES]
    for g in range(gpb):
        u_scr[g] = jnp.concatenate(
            [g_scr[g, pl.ds(k, rows, stride=sub), :] for k in range(sub)], axis=1).astype(BF16)

    for pair in range(npair):
        upair = jnp.concatenate([u_scr[2 * pair], u_scr[2 * pair + 1]], axis=1)
        sg = jnp.dot(upair, w_scr[pair], preferred_element_type=F32)
        for k in range(4):
            s_scr[:, k * sp + pair * LANES:k * sp + (pair + 1) * LANES] = sg[:, k * LANES:(k + 1) * LANES]

    nblk = nrows // SUBLANES
    row = lax.broadcasted_iota(jnp.int32, (SUBLANES, LANES), 0)

    def scan_block(pair, r0, d, carry):
        rows_ = slice(r0, r0 + SUBLANES)
        lanes_ = slice(pair * LANES, (pair + 1) * LANES)
        re_cols = slice(2 * d * sp + pair * LANES, 2 * d * sp + (pair + 1) * LANES)
        im_cols = slice((2 * d + 1) * sp + pair * LANES, (2 * d + 1) * sp + (pair + 1) * LANES)
        pr = s_scr[rows_, re_cols]
        pi = s_scr[rows_, im_cols]
        for step, sh in enumerate((1, 2, 4)):
            shift = sh if d == 0 else SUBLANES - sh
            tr, ti = _cmul(tab_ref[d, step, 0, :, lanes_], tab_ref[d, step, 1, :, lanes_],
                           pltpu.roll(pr, shift, 0), pltpu.roll(pi, shift, 0))
            pr = pr + tr
            pi = pi + ti
        cr, ci = carry
        tr, ti = _cmul(tab_ref[d, 3, 0, :, lanes_], tab_ref[d, 3, 1, :, lanes_], cr, ci)
        xr = pr + tr
        xi = pi + ti
        edge = 0 if d == 0 else SUBLANES - 1
        shift = 1 if d == 0 else SUBLANES - 1
        x_scr[rows_, re_cols] = jnp.where(row == edge, cr, pltpu.roll(xr, shift, 0))
        x_scr[rows_, im_cols] = jnp.where(row == edge, ci, pltpu.roll(xi, shift, 0))
        last = SUBLANES - 1 if d == 0 else 0
        return (jnp.broadcast_to(xr[last:last + 1, :], (SUBLANES, LANES)),
                jnp.broadcast_to(xi[last:last + 1, :], (SUBLANES, LANES)))

    zero = jnp.zeros((SUBLANES, LANES), F32)

    for pair in range(npair):
        for sq in range(nseq):
            cf = cb = (zero, zero)
            for i in range(nblk):
                cf = scan_block(pair, sq * nrows + i * SUBLANES, 0, cf)
                cb = scan_block(pair, sq * nrows + (nblk - 1 - i) * SUBLANES, 1, cb)
        xpair = jnp.concatenate(
            [x_scr[:, k * sp + pair * LANES:k * sp + (pair + 1) * LANES] for k in range(4)],
            axis=1).astype(BF16)
        cross = lax.dot_general(xpair, vt_scr[pair], (((1,), (1,)), ((), ())),
                                preferred_element_type=F32)
        for gi in range(2):
            g = 2 * pair + gi
            yg = (jnp.dot(u_scr[g], m_scr[g], preferred_element_type=F32)
                  + cross[:, gi * width:(gi + 1) * width])
            for k in range(sub):
                g_scr[g, pl.ds(k, rows, stride=sub), :] = yg[:, k * LANES:(k + 1) * LANES]

    ycat = jnp.concatenate([g_scr[g] for g in range(gpb)], axis=1).astype(BF16)
    ytok = jnp.dot(ycat, perm_ref[...], preferred_element_type=F32)
    for t in range(SUBLANES):
        y_ref[t] = ytok[:, t * LANES:(t + 1) * LANES].astype(BF16)


def _s5(u_blocks, mats, seq):
    perm, bt, ct, pw, dt0, tabs = mats
    nb, t, _ = u_blocks.shape
    t8 = t // SUBLANES
    c = SSM_CHUNK
    gpb = GROUPS_PER_BLOCK
    npair = gpb // 2
    nseq = SSM_SEQS
    while (t // seq) % nseq:
        nseq //= 2
    rows8 = nseq * seq // SUBLANES
    rows = nseq * seq // c
    width = c * SSM_GROUP
    sp = npair * LANES
    return pl.pallas_call(
        functools.partial(_s5_kernel, nseq=nseq, seq=seq),
        grid=(nb, t // (nseq * seq)),
        in_specs=[
            pl.BlockSpec((None, nseq * seq, LANES), lambda b, i: (b, i, 0)),
            _const_spec((SUBLANES * LANES, SUBLANES * LANES)),
            pl.BlockSpec((None, 2, gpb, 2, SSM_GROUP, LANES), lambda b, i: (b, 0, 0, 0, 0, 0)),
            pl.BlockSpec((None, 2, gpb, 2, SSM_GROUP, LANES), lambda b, i: (b, 0, 0, 0, 0, 0)),
            pl.BlockSpec((None, c + 1, 2, npair, 2, 1, LANES),
                         lambda b, i: (b, 0, 0, 0, 0, 0, 0)),
            pl.BlockSpec((None, gpb, SSM_GROUP, width), lambda b, i: (b, 0, 0, 0)),
            pl.BlockSpec((None, 2, 4, 2, SUBLANES, sp), lambda b, i: (b, 0, 0, 0, 0, 0)),
        ],
        out_specs=pl.BlockSpec((None, SUBLANES, rows8, LANES), lambda b, i: (b, 0, i, 0)),
        out_shape=jax.ShapeDtypeStruct((nb, SUBLANES, t8, LANES), BF16),
        scratch_shapes=[pltpu.VMEM((gpb, width, width), BF16),
                        pltpu.VMEM((npair, 2 * width, 4 * LANES), BF16),
                        pltpu.VMEM((npair, 2 * width, 4 * LANES), BF16),
                        pltpu.VMEM((gpb, rows8, LANES), F32),
                        pltpu.VMEM((gpb, rows, width), BF16),
                        pltpu.VMEM((rows, 4 * sp), F32),
                        pltpu.VMEM((rows, 4 * sp), F32)],
        compiler_params=pltpu.CompilerParams(
            dimension_semantics=("arbitrary", "arbitrary"), vmem_limit_bytes=VMEM_LIMIT),
        name="s5",
    )(u_blocks, perm, bt, ct, pw, dt0, tabs)


def _s5_tables(lam_re, lam_im, log_dt, b_re, b_im, c_re, c_im, d_skip):
    c = SSM_CHUNK
    nb, gpb = SSM_BLOCKS, GROUPS_PER_BLOCK
    npair = gpb // 2
    lr = jnp.minimum(lam_re.astype(F32), -1e-4)
    li = lam_im.astype(F32)
    dt = jnp.exp(log_dt.astype(F32))[..., None]
    plist = list(range(c + 1)) + [c * k for k in range(2, SUBLANES + 1)]
    pidx = {pw_: i for i, pw_ in enumerate(plist)}
    p = jnp.asarray(plist, F32)[:, None, None, None]
    pmag = jnp.exp(p * (lr * dt)[None])
    pr = pmag * jnp.cos(p * (li * dt)[None])
    pi = pmag * jnp.sin(p * (li * dt)[None])
    ar, ai = pr[1], pi[1]
    den = lr * lr + li * li
    nr = ar - 1.0
    ni = ai
    cr = ((nr * lr + ni * li) / den)[:, :, None, :]
    ci = ((ni * lr - nr * li) / den)[:, :, None, :]
    b_re_t = b_re.astype(F32).transpose(0, 1, 3, 2)
    b_im_t = b_im.astype(F32).transpose(0, 1, 3, 2)
    bbar = jnp.stack([cr * b_re_t - ci * b_im_t, cr * b_im_t + ci * b_re_t], axis=2)
    cmat = jnp.stack([c_re.astype(F32), c_im.astype(F32)], axis=2)

    def half_placed(x):
        own = (jnp.arange(SSM_GROUPS) % 2)[:, None] == jnp.arange(2)[None, :]
        wide = jnp.where(own[None, :, None, None, :, None], x[:, :, :, :, None, :], 0.0)
        wide = wide.reshape(2, nb, gpb, 2, SSM_GROUP, LANES)
        return jnp.moveaxis(wide, 1, 0)

    pows = jnp.stack([pr, pi], axis=3)[:c + 1]
    pw = pows.reshape(c + 1, 2, nb, npair, 2, 2, SSM_STATE).transpose(2, 0, 1, 3, 5, 4, 6)
    pw = pw.reshape(nb, c + 1, 2, npair, 2, 1, LANES)

    lane = jnp.arange(c * SSM_GROUP)
    dt0 = jnp.where(lane[None, None, :] == jnp.arange(SSM_GROUP)[None, :, None],
                    d_skip.astype(F32).reshape(SSM_GROUPS, SSM_GROUP, 1), 0.0)
    dt0 = dt0.reshape(nb, gpb, SSM_GROUP, c * SSM_GROUP)

    rowi = jnp.arange(SUBLANES)

    def lanes(x):
        return x.reshape(x.shape[:-2] + (nb, npair * LANES))

    tabs = []
    for d in range(2):
        per_dir = []
        for sh in (1, 2, 4):
            keep = (rowi >= sh) if d == 0 else (rowi < SUBLANES - sh)
            per_dir.append(jnp.stack([
                jnp.where(keep[:, None, None], lanes(pr[pidx[sh * c], d])[None], 0.0),
                jnp.where(keep[:, None, None], lanes(pi[pidx[sh * c], d])[None], 0.0)]))
        mult = range(1, SUBLANES + 1) if d == 0 else range(SUBLANES, 0, -1)
        cpow = jnp.asarray([pidx[k * c] for k in mult])
        per_dir.append(jnp.stack([lanes(pr[cpow, d]), lanes(pi[cpow, d])]))
        tabs.append(jnp.stack(per_dir))
    tabs = jnp.stack(tabs).transpose(4, 0, 1, 2, 3, 5)

    src = jnp.arange(SUBLANES * LANES)
    dst = ((src >> 4) & 7) * LANES + (src >> 7) * SSM_GROUP + (src & 15)
    perm = (dst[:, None] == src[None, :]).astype(BF16)
    return perm, half_placed(bbar), half_placed(cmat), pw, dt0, tabs


def _mix_norm_kernel(x_ref, ret_ref, ssm_ref, glu_w_ref, glu_b_ref, w_out_ref, nffn_ref,
                     mix_ref, h_ref, tok_scr):
    rows8 = ret_ref.shape[0] // SUBLANES
    for b in range(SSM_BLOCKS):
        for t in range(SUBLANES):
            tok_scr[b, pl.ds(t, rows8, stride=SUBLANES), :] = ssm_ref[b, t].astype(F32)
    ssm = jnp.concatenate([tok_scr[b] for b in range(SSM_BLOCKS)], axis=1)
    y = jax.nn.gelu(ssm)
    gate = jax.nn.sigmoid(
        jnp.dot(y.astype(BF16), glu_w_ref[...], preferred_element_type=F32) + glu_b_ref[...])
    cat = jnp.concatenate([ret_ref[...], (y * gate).astype(BF16)], axis=1)
    mix = jnp.dot(cat, w_out_ref[...], preferred_element_type=F32).astype(BF16)
    mix_ref[...] = mix
    h_ref[...] = _rms(x_ref[...] + mix.astype(F32), nffn_ref[...]).astype(BF16)


def _mix_norm(x2, ret, ssm, glu_w, glu_b, w_out, norm_ffn):
    t = x2.shape[0]
    r = MIX_TILE
    tok = lambda i: (i, 0)
    return pl.pallas_call(
        _mix_norm_kernel,
        grid=(t // r,),
        in_specs=[pl.BlockSpec((r, D_MODEL), tok), pl.BlockSpec((r, RET_WIDTH), tok),
                  pl.BlockSpec((SSM_BLOCKS, SUBLANES, r // SUBLANES, LANES),
                               lambda i: (0, 0, i, 0)),
                  _const_spec((SSM_WIDTH, SSM_WIDTH)), _const_spec((1, SSM_WIDTH)),
                  _const_spec((D_MODEL, D_MODEL)), _const_spec((1, D_MODEL))],
        out_specs=[pl.BlockSpec((r, D_MODEL), tok), pl.BlockSpec((r, D_MODEL), tok)],
        out_shape=[jax.ShapeDtypeStruct((t, D_MODEL), BF16),
                   jax.ShapeDtypeStruct((t, D_MODEL), BF16)],
        scratch_shapes=[pltpu.VMEM((SSM_BLOCKS, r, LANES), F32)],
        compiler_params=pltpu.CompilerParams(
            dimension_semantics=("arbitrary",), vmem_limit_bytes=VMEM_LIMIT),
        name="mix_norm",
    )(x2, ret, ssm, glu_w, glu_b, w_out, norm_ffn)


def _conv_ffn_kernel(x_ref, mix_ref, hm_ref, hp_ref, hn_ref, w_up_ref, cw_ref, cb_ref,
                     w_down_ref, nfin_ref, o_ref, hid_scr, *, tiles_per_seq):
    r = hm_ref.shape[0]
    ext = r + 2 * HALO
    j = pl.program_id(0) % tiles_per_seq
    hp = jnp.where(j == 0, jnp.zeros(hp_ref.shape, BF16), hp_ref[...])
    hn = jnp.where(j == tiles_per_seq - 1, jnp.zeros(hn_ref.shape, BF16), hn_ref[...])
    h = jnp.concatenate([hp, hm_ref[...], hn], axis=0)
    for jj in range(D_FF // FF_TILE):
        parts = []
        for off in (jj * FF_TILE, D_FF + jj * FF_TILE):
            z = jnp.dot(h, w_up_ref[:, off:off + FF_TILE], preferred_element_type=F32)
            zc = (pltpu.roll(z, 1, 0)[HALO:HALO + r] * cw_ref[0:1, off:off + FF_TILE]
                  + z[HALO:HALO + r] * cw_ref[1:2, off:off + FF_TILE]
                  + pltpu.roll(z, ext - 1, 0)[HALO:HALO + r] * cw_ref[2:3, off:off + FF_TILE]
                  + cb_ref[:, off:off + FF_TILE])
            parts.append(zc)
        val, gate = parts
        t = jnp.tanh(gate * (GELU_C0 + (GELU_C0 * GELU_C1) * (gate * gate)))
        hid_scr[:, jj * FF_TILE:(jj + 1) * FF_TILE] = ((0.5 * gate * val) * (1.0 + t)).astype(BF16)
    x1 = x_ref[...] + mix_ref[...].astype(F32)
    x2 = x1 + jnp.dot(hid_scr[...], w_down_ref[...], preferred_element_type=F32)
    o_ref[...] = _rms(x2, nfin_ref[...])


def _conv_ffn(x, mix, h, w_up, conv_w, conv_b, w_down, norm_final, seq):
    t = x.shape[0]
    r = FFN_TILE
    assert seq % r == 0, "token tiles must not straddle sequences"
    tiles_per_seq = seq // r
    hb = r // HALO
    last = t // HALO - 1
    tok = lambda i: (i, 0)
    return pl.pallas_call(
        functools.partial(_conv_ffn_kernel, tiles_per_seq=tiles_per_seq),
        grid=(t // r,),
        in_specs=[pl.BlockSpec((r, D_MODEL), tok), pl.BlockSpec((r, D_MODEL), tok),
                  pl.BlockSpec((r, D_MODEL), tok),
                  pl.BlockSpec((HALO, D_MODEL), lambda i: (jnp.maximum(i * hb - 1, 0), 0)),
                  pl.BlockSpec((HALO, D_MODEL), lambda i: (jnp.minimum((i + 1) * hb, last), 0)),
                  _const_spec((D_MODEL, 2 * D_FF)), _const_spec((3, 2 * D_FF)),
                  _const_spec((1, 2 * D_FF)), _const_spec((D_FF, D_MODEL)),
                  _const_spec((1, D_MODEL))],
        out_specs=pl.BlockSpec((r, D_MODEL), tok),
        out_shape=jax.ShapeDtypeStruct((t, D_MODEL), F32),
        scratch_shapes=[pltpu.VMEM((r, D_FF), BF16)],
        compiler_params=pltpu.CompilerParams(
            dimension_semantics=("arbitrary",), vmem_limit_bytes=VMEM_LIMIT),
        name="conv_ffn",
    )(x, mix, h, h, h, w_up, conv_w, conv_b, w_down, norm_final)


def _rotary_tables(seq):
    half = HEAD_DIM // 2
    inv_freq = ROPE_BASE ** (-jnp.arange(half, dtype=F32) / half)
    ang = jnp.arange(seq, dtype=F32)[:, None] * inv_freq[None, :]
    cos = jnp.cos(ang)
    sin = jnp.sin(ang)
    return jnp.concatenate([cos, cos], axis=1), jnp.concatenate([-sin, sin], axis=1)


def _layer(x, prep):
    b, seq, _ = x.shape
    x2 = x.reshape(b * seq, D_MODEL)
    q, k, v, g, u = _in_proj(x2, prep["norm_mix"], prep["w_in"], prep["cos"], prep["sin"],
                             prep["gn_gain"], seq)
    ret = _retention(q, k, v, g, prep["ret_tabs"], seq)
    ssm = _s5(u, prep["s5"], seq)
    mix, h = _mix_norm(x2, ret, ssm, prep["glu_w"], prep["glu_b"], prep["w_out"], prep["norm_ffn"])
    out = _conv_ffn(x2, mix, h, prep["w_up"], prep["conv_w"], prep["conv_b"], prep["w_down"],
                    prep["norm_final"], seq)
    return out.reshape(b, seq, D_MODEL)


def kernel(x_prompt, x_sample, norm_mix, w_in, ret_gn_gain, s5_lambda_re, s5_lambda_im, s5_log_dt, s5_B_re, s5_B_im, s5_C_re, s5_C_im, s5_D, s5_glu_w, s5_glu_b, w_out, norm_ffn, w_up, conv_w, conv_b, w_down, norm_final):
    assert norm_mix.shape[0] == 1, "single-layer trunk"
    seq = x_prompt.shape[1]
    assert x_sample.shape[1] == seq
    cos, sin = _rotary_tables(seq)
    prep = {
        "norm_mix": norm_mix[0][None].astype(F32),
        "w_in": w_in[0].astype(BF16),
        "cos": cos, "sin": sin,
        "ret_tabs": _retention_tables(),
        "gn_gain": ret_gn_gain[0][None].astype(F32),
        "s5": _s5_tables(s5_lambda_re[0], s5_lambda_im[0], s5_log_dt[0], s5_B_re[0], s5_B_im[0],
                         s5_C_re[0], s5_C_im[0], s5_D[0]),
        "glu_w": s5_glu_w[0].astype(BF16),
        "glu_b": s5_glu_b[0][None].astype(F32),
        "w_out": w_out[0].astype(BF16),
        "norm_ffn": norm_ffn[0][None].astype(F32),
        "w_up": w_up[0].astype(BF16),
        "conv_w": conv_w[0].astype(F32),
        "conv_b": conv_b[0][None].astype(F32),
        "w_down": w_down[0].astype(BF16),
        "norm_final": norm_final[None].astype(F32),
    }
    return (_layer(x_prompt, prep), _layer(x_sample, prep))
```

```python
import functools

import jax
import jax.numpy as jnp
import numpy as np
from jax import lax
from jax.experimental import pallas as pl
from jax.experimental.pallas import tpu as pltpu

D_MODEL = 1024
RET_WIDTH = 512
SSM_WIDTH = 512
RET_HEADS = 4
HEAD_DIM = 128
ROPE_BASE = 10000.0
SSM_GROUP = 16
SSM_GROUPS = 32
SSM_STATE = 64
D_FF = 2816
EPS = 1e-6
GELU_C0 = 0.7978845608028654
GELU_C1 = 0.044715
IN_WIDTH = 4 * RET_WIDTH + SSM_WIDTH

LANES = 128
SUBLANES = 8
BF16_ROWS = 16
VMEM_LIMIT = 56 * 1024 * 1024

RET_BLOCK = 256
SSM_BLOCKS = SSM_WIDTH // LANES
GROUPS_PER_BLOCK = LANES // SSM_GROUP
SSM_CHUNK = 16
SSM_SEQS = 4
TOK_TILE = 1024
FFN_TILE = 1024
MIX_TILE = 1024
FF_TILE = 256
HALO = BF16_ROWS

BF16 = jnp.bfloat16
F32 = jnp.float32


def _rms(x, w):
    return x * lax.rsqrt(jnp.mean(x * x, axis=-1, keepdims=True) + EPS) * w


def _const_spec(shape):
    nd = len(shape)
    return pl.BlockSpec(shape, lambda *_: (0,) * nd, pipeline_mode=pl.Buffered(1))


def _in_proj_kernel(x_ref, nw_ref, w_ref, cos_ref, sin_ref, gain_ref,
                    q_ref, k_ref, v_ref, g_ref, u_ref):
    h = _rms(x_ref[...], nw_ref[...]).astype(BF16)
    proj = jnp.dot(h, w_ref[...], preferred_element_type=F32)
    cos = cos_ref[...]
    sin = sin_ref[...]
    kscale = HEAD_DIM ** -0.5
    for hd in range(RET_HEADS):
        lo = hd * HEAD_DIM
        qh = proj[:, lo:lo + HEAD_DIM]
        q_ref[:, lo:lo + HEAD_DIM] = (qh * cos + pltpu.roll(qh, HEAD_DIM // 2, 1) * sin).astype(BF16)
        kh = proj[:, RET_WIDTH + lo:RET_WIDTH + lo + HEAD_DIM]
        kr = kh * cos + pltpu.roll(kh, HEAD_DIM // 2, 1) * sin
        k_ref[:, lo:lo + HEAD_DIM] = (kr * kscale).astype(BF16)
    v_ref[...] = proj[:, 2 * RET_WIDTH:3 * RET_WIDTH].astype(BF16)
    g_ref[...] = (jax.nn.silu(proj[:, 3 * RET_WIDTH:4 * RET_WIDTH]) * gain_ref[...]).astype(BF16)
    for blk in range(SSM_BLOCKS):
        lo = 4 * RET_WIDTH + blk * LANES
        u_ref[blk] = proj[:, lo:lo + LANES]


def _in_proj(x2, norm_w, w_in, cos2, sin2, gain, seq):
    t = x2.shape[0]
    r = TOK_TILE
    assert seq % r == 0, "token tiles must not straddle sequences"
    tiles_per_seq = seq // r
    tok = lambda i: (i, 0)
    out_tok = jax.ShapeDtypeStruct((t, RET_WIDTH), BF16)
    return pl.pallas_call(
        _in_proj_kernel,
        grid=(t // r,),
        in_specs=[
            pl.BlockSpec((r, D_MODEL), tok),
            _const_spec((1, D_MODEL)),
            _const_spec((D_MODEL, IN_WIDTH)),
            pl.BlockSpec((r, HEAD_DIM), lambda i: (i % tiles_per_seq, 0)),
            pl.BlockSpec((r, HEAD_DIM), lambda i: (i % tiles_per_seq, 0)),
            _const_spec((1, RET_WIDTH)),
        ],
        out_specs=[
            pl.BlockSpec((r, RET_WIDTH), tok),
            pl.BlockSpec((r, RET_WIDTH), tok),
            pl.BlockSpec((r, RET_WIDTH), tok),
            pl.BlockSpec((r, RET_WIDTH), tok),
            pl.BlockSpec((SSM_BLOCKS, r, LANES), lambda i: (0, i, 0)),
        ],
        out_shape=[out_tok, out_tok, out_tok, out_tok,
                   jax.ShapeDtypeStruct((SSM_BLOCKS, t, LANES), F32)],
        compiler_params=pltpu.CompilerParams(
            dimension_semantics=("arbitrary",), vmem_limit_bytes=VMEM_LIMIT),
        name="in_proj",
    )(x2, norm_w, w_in, cos2, sin2, gain)


def _retention_kernel(q_ref, k_ref, v_ref, g_ref, dmat_ref, cd_ref, row_ref, o_ref,
                      kv_scr, r_scr):
    c = RET_BLOCK
    nchunks = q_ref.shape[0] // c
    heads = [slice(hd * HEAD_DIM, (hd + 1) * HEAD_DIM) for hd in range(RET_HEADS)]

    for n in range(nchunks):
        rows = slice(n * c, (n + 1) * c)
        for hd, cols in enumerate(heads):
            kn = k_ref[rows, cols]
            kcat = jnp.concatenate([kn * row_ref[hd, 0], kn * row_ref[hd, 1]], axis=1)
            kv_scr[n, hd] = lax.dot_general(kcat, v_ref[rows, cols], (((0,), (0,)), ((), ())),
                                            preferred_element_type=F32)

    for hd in range(RET_HEADS):
        cd = cd_ref[hd]
        rf = jnp.zeros((HEAD_DIM, HEAD_DIM), F32)
        for n in range(nchunks):
            r_scr[n, hd, 0:HEAD_DIM, :] = rf.astype(BF16)
            rf = cd * rf + kv_scr[n, hd, 0:HEAD_DIM, :]
        rb = jnp.zeros((HEAD_DIM, HEAD_DIM), F32)
        for n in reversed(range(nchunks)):
            r_scr[n, hd, HEAD_DIM:2 * HEAD_DIM, :] = rb.astype(BF16)
            rb = cd * rb + kv_scr[n, hd, HEAD_DIM:2 * HEAD_DIM, :]

    for n in range(nchunks):
        rows = slice(n * c, (n + 1) * c)
        for hd, cols in enumerate(heads):
            qn = q_ref[rows, cols]
            s = lax.dot_general(qn, k_ref[rows, cols], (((1,), (1,)), ((), ())),
                                preferred_element_type=F32) * dmat_ref[hd]
            inner = jnp.dot(s.astype(BF16), v_ref[rows, cols], preferred_element_type=F32)
            qcat = jnp.concatenate([qn * row_ref[hd, 2], qn * row_ref[hd, 3]], axis=1)
            o = inner + jnp.dot(qcat, r_scr[n, hd], preferred_element_type=F32)
            o = o * lax.rsqrt(jnp.mean(o * o, axis=-1, keepdims=True) + EPS)
            o_ref[rows, cols] = (g_ref[rows, cols].astype(F32) * o).astype(BF16)


def _retention(q, k, v, g, tabs, seq):
    dmat, cd, tab_rows = tabs
    t = q.shape[0]
    nchunks = seq // RET_BLOCK
    blk = pl.BlockSpec((seq, RET_WIDTH), lambda b: (b, 0))
    return pl.pallas_call(
        _retention_kernel,
        grid=(t // seq,),
        in_specs=[blk, blk, blk, blk,
                  _const_spec((RET_HEADS, RET_BLOCK, RET_BLOCK)),
                  _const_spec((RET_HEADS, HEAD_DIM, HEAD_DIM)),
                  _const_spec((RET_HEADS, 4, RET_BLOCK, HEAD_DIM))],
        out_specs=blk,
        out_shape=jax.ShapeDtypeStruct((t, RET_WIDTH), BF16),
        scratch_shapes=[pltpu.VMEM((nchunks, RET_HEADS, 2 * HEAD_DIM, HEAD_DIM), F32),
                        pltpu.VMEM((nchunks, RET_HEADS, 2 * HEAD_DIM, HEAD_DIM), BF16)],
        compiler_params=pltpu.CompilerParams(
            dimension_semantics=("arbitrary",), vmem_limit_bytes=VMEM_LIMIT),
        name="retention",
    )(q, k, v, g, dmat, cd, tab_rows)


def _retention_tables():
    c = RET_BLOCK
    lg = jnp.log(1.0 - 2.0 ** (-5.0 - jnp.arange(RET_HEADS, dtype=F32)))[:, None, None]
    idx = jnp.arange(c, dtype=F32)
    col = jnp.broadcast_to(idx[:, None], (c, HEAD_DIM))[None]
    dmat = jnp.exp(lg * jnp.abs(idx[:, None] - idx[None, :])[None])
    cd = jnp.exp(lg * c) * jnp.ones((1, HEAD_DIM, HEAD_DIM), F32)
    zf = jnp.exp(lg * (c - 1.0 - col))
    zb = jnp.exp(lg * col)
    xf = jnp.exp(lg * (col + 1.0))
    xb = jnp.exp(lg * (c - col))
    return dmat, cd, jnp.stack([zf, zb, xf, xb], axis=1).astype(BF16)


def _cmul(ar, ai, br, bi):
    return ar * br - ai * bi, ar * bi + ai * br


def _roll_lanes(x, shift):
    ntiles = x.shape[1] // LANES
    whole, part = divmod(shift % x.shape[1], LANES)
    tiles = [x[:, k * LANES:(k + 1) * LANES] for k in range(ntiles)]
    tiles = [tiles[(k - whole) % ntiles] for k in range(ntiles)]
    if part:
        lane = lax.broadcasted_iota(jnp.int32, tiles[0].shape, 1)
        rolled = [pltpu.roll(t, part, 1) for t in tiles]
        tiles = [jnp.where(lane < part, rolled[(k - 1) % ntiles], rolled[k]) for k in range(ntiles)]
    return tiles[0] if ntiles == 1 else jnp.concatenate(tiles, axis=1)


def _split_dot(a, z):
    dims = (((1,), (1,)), ((), ()))
    a_hi = a.astype(BF16)
    z_hi = z.astype(BF16)
    a_lo = (a - a_hi.astype(F32)).astype(BF16)
    z_lo = (z - z_hi.astype(F32)).astype(BF16)
    return (lax.dot_general(a_hi, z_hi, dims, preferred_element_type=F32)
            + lax.dot_general(a_hi, z_lo, dims, preferred_element_type=F32)
            + lax.dot_general(a_lo, z_hi, dims, preferred_element_type=F32))


def _s5_kernel(u_ref, perm_ref, bt_ref, ct_ref, pw_ref, dt0_ref, tab_ref, y_ref,
               m_scr, w_scr, vt_scr, g_scr, u_scr, s_scr, x_scr, *, nseq, seq):
    c = SSM_CHUNK
    sub = c // SUBLANES
    gpb = GROUPS_PER_BLOCK
    npair = gpb // 2
    width = c * SSM_GROUP
    sp = npair * LANES
    rows8 = nseq * seq // SUBLANES
    rows = rows8 // sub
    nrows = seq // c

    @pl.when(pl.program_id(1) == 0)
    def _expand():
        slot = lax.broadcasted_iota(jnp.int32, (SSM_GROUP, width), 1) >> 4
        for g in range(gpb):
            pair, gi = divmod(g, 2)
            zt = ([], [])
            for d in range(2):
                b_re, b_im = bt_ref[d, g, 0], bt_ref[d, g, 1]
                c_re, c_im = ct_ref[d, g, 0], ct_ref[d, g, 1]
                for s in range(c):
                    r16 = slice(gi * width + s * SSM_GROUP, gi * width + (s + 1) * SSM_GROUP)
                    re_cols = slice((2 * d) * LANES, (2 * d + 1) * LANES)
                    im_cols = slice((2 * d + 1) * LANES, (2 * d + 2) * LANES)
                    p_w = c - 1 - s if d == 0 else s
                    p_v = s + 1 if d == 0 else c - s
                    p_z = s if d == 0 else c - 1 - s
                    wr, wi = _cmul(b_re, b_im, pw_ref[p_w, d, pair, 0], pw_ref[p_w, d, pair, 1])
                    w_scr[pair, r16, re_cols] = wr.astype(BF16)
                    w_scr[pair, r16, im_cols] = wi.astype(BF16)
                    er, ei = _cmul(c_re, c_im, pw_ref[p_v, d, pair, 0], pw_ref[p_v, d, pair, 1])
                    vt_scr[pair, r16, re_cols] = er.astype(BF16)
                    vt_scr[pair, r16, im_cols] = (-ei).astype(BF16)
                    er, ei = _cmul(c_re, c_im, pw_ref[p_z, d, pair, 0], pw_ref[p_z, d, pair, 1])
                    zt[d].append(jnp.concatenate([er, -ei], axis=1))
            bcat = [jnp.concatenate([bt_ref[d, g, 0], bt_ref[d, g, 1]], axis=1) for d in range(2)]
            f0 = _split_dot(bcat[0], jnp.concatenate(zt[0], axis=0)) + dt0_ref[g]
            b0 = _split_dot(bcat[1], jnp.concatenate(zt[1], axis=0))
            for s in range(c):
                fwd = jnp.where(slot >= s, _roll_lanes(f0, SSM_GROUP * s), 0.0)
                bwd = jnp.where(slot <= s, _roll_lanes(b0, SSM_GROUP * (s + 1)), 0.0)
                m_scr[g, s * SSM_GROUP:(s + 1) * SSM_GROUP, :] = (fwd + bwd).astype(BF16)

    ucat = jnp.concatenate(
        [u_ref[pl.ds(t, rows8, stride=SUBLANES), :].astype(BF16) for t in range(SUBLANES)], axis=1)
    grouped = jnp.dot(ucat, perm_ref[...], preferred_element_type=F32)
    for g in range(gpb):
        g_scr[g] = grouped[:, g * LANES:(g + 1) * LANES]
    for g in range(gpb):
        u_scr[g] = jnp.concatenate(
            [g_scr[g, pl.ds(k, rows, stride=sub), :] for k in range(sub)], axis=1).astype(BF16)

    for pair in range(npair):
        upair = jnp.concatenate([u_scr[2 * pair], u_scr[2 * pair + 1]], axis=1)
        sg = jnp.dot(upair, w_scr[pair], preferred_element_type=F32)
        for k in range(4):
            s_scr[:, k * sp + pair * LANES:k * sp + (pair + 1) * LANES] = sg[:, k * LANES:(k + 1) * LANES]

    nblk = nrows // SUBLANES
    row = lax.broadcasted_iota(jnp.int32, (SUBLANES, LANES), 0)

    def scan_block(pair, r0, d, carry):
        rows_ = slice(r0, r0 + SUBLANES)
        lanes_ = slice(pair * LANES, (pair + 1) * LANES)
        re_cols = slice(2 * d * sp + pair * LANES, 2 * d * sp + (pair + 1) * LANES)
        im_cols = slice((2 * d + 1) * sp + pair * LANES, (2 * d + 1) * sp + (pair + 1) * LANES)
        pr = s_scr[rows_, re_cols]
        pi = s_scr[rows_, im_cols]
        for step, sh in enumerate((1, 2, 4)):
            shift = sh if d == 0 else SUBLANES - sh
            tr, ti = _cmul(tab_ref[d, step, 0, :, lanes_], tab_ref[d, step, 1, :, lanes_],
                           pltpu.roll(pr, shift, 0), pltpu.roll(pi, shift, 0))
            pr = pr + tr
            pi = pi + ti
        cr, ci = carry
        tr, ti = _cmul(tab_ref[d, 3, 0, :, lanes_], tab_ref[d, 3, 1, :, lanes_], cr, ci)
        xr = pr + tr
        xi = pi + ti
        edge = 0 if d == 0 else SUBLANES - 1
        shift = 1 if d == 0 else SUBLANES - 1
        x_scr[rows_, re_cols] = jnp.where(row == edge, cr, pltpu.roll(xr, shift, 0))
        x_scr[rows_, im_cols] = jnp.where(row == edge, ci, pltpu.roll(xi, shift, 0))
        last = SUBLANES - 1 if d == 0 else 0
        return (jnp.broadcast_to(xr[last:last + 1, :], (SUBLANES, LANES)),
                jnp.broadcast_to(xi[last:last + 1, :], (SUBLANES, LANES)))

    zero = jnp.zeros((SUBLANES, LANES), F32)

    for pair in range(npair):
        for sq in range(nseq):
            cf = cb = (zero, zero)
            for i in range(nblk):
                cf = scan_block(pair, sq * nrows + i * SUBLANES, 0, cf)
                cb = scan_block(pair, sq * nrows + (nblk - 1 - i) * SUBLANES, 1, cb)
        xpair = jnp.concatenate(
            [x_scr[:, k * sp + pair * LANES:k * sp + (pair + 1) * LANES] for k in range(4)],
            axis=1).astype(BF16)
        cross = lax.dot_general(xpair, vt_scr[pair], (((1,), (1,)), ((), ())),
                                preferred_element_type=F32)
        for gi in range(2):
            g = 2 * pair + gi
            yg = (jnp.dot(u_scr[g], m_scr[g], preferred_element_type=F32)
                  + cross[:, gi * width:(gi + 1) * width])
            for k in range(sub):
                g_scr[g, pl.ds(k, rows, stride=sub), :] = yg[:, k * LANES:(k + 1) * LANES]

    ycat = jnp.concatenate([g_scr[g] for g in range(gpb)], axis=1).astype(BF16)
    ytok = jnp.dot(ycat, perm_ref[...], preferred_element_type=F32)
    for t in range(SUBLANES):
        y_ref[t] = ytok[:, t * LANES:(t + 1) * LANES].astype(BF16)


def _s5(u_blocks, mats, seq):
    perm, bt, ct, pw, dt0, tabs = mats
    nb, t, _ = u_blocks.shape
    t8 = t // SUBLANES
    c = SSM_CHUNK
    gpb = GROUPS_PER_BLOCK
    npair = gpb // 2
    nseq = SSM_SEQS
    while (t // seq) % nseq:
        nseq //= 2
    rows8 = nseq * seq // SUBLANES
    rows = nseq * seq // c
    width = c * SSM_GROUP
    sp = npair * LANES
    return pl.pallas_call(
        functools.partial(_s5_kernel, nseq=nseq, seq=seq),
        grid=(nb, t // (nseq * seq)),
        in_specs=[
            pl.BlockSpec((None, nseq * seq, LANES), lambda b, i: (b, i, 0)),
            _const_spec((SUBLANES * LANES, SUBLANES * LANES)),
            pl.BlockSpec((None, 2, gpb, 2, SSM_GROUP, LANES), lambda b, i: (b, 0, 0, 0, 0, 0)),
            pl.BlockSpec((None, 2, gpb, 2, SSM_GROUP, LANES), lambda b, i: (b, 0, 0, 0, 0, 0)),
            pl.BlockSpec((None, c + 1, 2, npair, 2, 1, LANES),
                         lambda b, i: (b, 0, 0, 0, 0, 0, 0)),
            pl.BlockSpec((None, gpb, SSM_GROUP, width), lambda b, i: (b, 0, 0, 0)),
            pl.BlockSpec((None, 2, 4, 2, SUBLANES, sp), lambda b, i: (b, 0, 0, 0, 0, 0)),
        ],
        out_specs=pl.BlockSpec((None, SUBLANES, rows8, LANES), lambda b, i: (b, 0, i, 0)),
        out_shape=jax.ShapeDtypeStruct((nb, SUBLANES, t8, LANES), BF16),
        scratch_shapes=[pltpu.VMEM((gpb, width, width), BF16),
                        pltpu.VMEM((npair, 2 * width, 4 * LANES), BF16),
                        pltpu.VMEM((npair, 2 * width, 4 * LANES), BF16),
                        pltpu.VMEM((gpb, rows8, LANES), F32),
                        pltpu.VMEM((gpb, rows, width), BF16),
                        pltpu.VMEM((rows, 4 * sp), F32),
                        pltpu.VMEM((rows, 4 * sp), F32)],
        compiler_params=pltpu.CompilerParams(
            dimension_semantics=("arbitrary", "arbitrary"), vmem_limit_bytes=VMEM_LIMIT),
        name="s5",
    )(u_blocks, perm, bt, ct, pw, dt0, tabs)


def _s5_tables(lam_re, lam_im, log_dt, b_re, b_im, c_re, c_im, d_skip):
    c = SSM_CHUNK
    nb, gpb = SSM_BLOCKS, GROUPS_PER_BLOCK
    npair = gpb // 2
    lr = jnp.minimum(lam_re.astype(F32), -1e-4)
    li = lam_im.astype(F32)
    dt = jnp.exp(log_dt.astype(F32))[..., None]
    plist = list(range(c + 1)) + [c * k for k in range(2, SUBLANES + 1)]
    pidx = {pw_: i for i, pw_ in enumerate(plist)}
    p = jnp.asarray(plist, F32)[:, None, None, None]
    pmag = jnp.exp(p * (lr * dt)[None])
    pr = pmag * jnp.cos(p * (li * dt)[None])
    pi = pmag * jnp.sin(p * (li * dt)[None])
    ar, ai = pr[1], pi[1]
    den = lr * lr + li * li
    nr = ar - 1.0
    ni = ai
    cr = ((nr * lr + ni * li) / den)[:, :, None, :]
    ci = ((ni * lr - nr * li) / den)[:, :, None, :]
    b_re_t = b_re.astype(F32).transpose(0, 1, 3, 2)
    b_im_t = b_im.astype(F32).transpose(0, 1, 3, 2)
    bbar = jnp.stack([cr * b_re_t - ci * b_im_t, cr * b_im_t + ci * b_re_t], axis=2)
    cmat = jnp.stack([c_re.astype(F32), c_im.astype(F32)], axis=2)

    def half_placed(x):
        own = (jnp.arange(SSM_GROUPS) % 2)[:, None] == jnp.arange(2)[None, :]
        wide = jnp.where(own[None, :, None, None, :, None], x[:, :, :, :, None, :], 0.0)
        wide = wide.reshape(2, nb, gpb, 2, SSM_GROUP, LANES)
        return jnp.moveaxis(wide, 1, 0)

    pows = jnp.stack([pr, pi], axis=3)[:c + 1]
    pw = pows.reshape(c + 1, 2, nb, npair, 2, 2, SSM_STATE).transpose(2, 0, 1, 3, 5, 4, 6)
    pw = pw.reshape(nb, c + 1, 2, npair, 2, 1, LANES)

    lane = jnp.arange(c * SSM_GROUP)
    dt0 = jnp.where(lane[None, None, :] == jnp.arange(SSM_GROUP)[None, :, None],
                    d_skip.astype(F32).reshape(SSM_GROUPS, SSM_GROUP, 1), 0.0)
    dt0 = dt0.reshape(nb, gpb, SSM_GROUP, c * SSM_GROUP)

    rowi = np.arange(SUBLANES)
    shifts = (1, 2, 4)
    keep = np.stack([np.stack([rowi >= sh for sh in shifts]),
                     np.stack([rowi < SUBLANES - sh for sh in shifts])])
    sh_idx = np.array([pidx[sh * c] for sh in shifts])
    carry_idx = np.array([[pidx[k * c] for k in range(1, SUBLANES + 1)],
                          [pidx[k * c] for k in range(SUBLANES, 0, -1)]])

    def scan_table(pp):
        steps = jnp.where(keep[:, :, :, None, None],
                          jnp.moveaxis(pp[sh_idx], 1, 0)[:, :, None], 0.0)
        carry = pp[carry_idx, np.arange(2)[:, None]]
        return jnp.concatenate([steps, carry[:, None]], axis=1)

    tabs = jnp.stack([scan_table(pr), scan_table(pi)], axis=2)
    tabs = tabs.reshape(2, 4, 2, SUBLANES, nb, npair * LANES).transpose(4, 0, 1, 2, 3, 5)

    src = jnp.arange(SUBLANES * LANES)
    dst = ((src >> 4) & 7) * LANES + (src >> 7) * SSM_GROUP + (src & 15)
    perm = (dst[:, None] == src[None, :]).astype(BF16)
    return perm, half_placed(bbar), half_placed(cmat), pw, dt0, tabs


def _mix_norm_kernel(x_ref, ret_ref, ssm_ref, glu_w_ref, glu_b_ref, w_out_ref, nffn_ref,
                     mix_ref, h_ref, tok_scr):
    rows8 = ret_ref.shape[0] // SUBLANES
    for b in range(SSM_BLOCKS):
        for t in range(SUBLANES):
            tok_scr[b, pl.ds(t, rows8, stride=SUBLANES), :] = ssm_ref[b, t].astype(F32)
    ssm = jnp.concatenate([tok_scr[b] for b in range(SSM_BLOCKS)], axis=1)
    y = jax.nn.gelu(ssm)
    gate = jax.nn.sigmoid(
        jnp.dot(y.astype(BF16), glu_w_ref[...], preferred_element_type=F32) + glu_b_ref[...])
    cat = jnp.concatenate([ret_ref[...], (y * gate).astype(BF16)], axis=1)
    mix = jnp.dot(cat, w_out_ref[...], preferred_element_type=F32).astype(BF16)
    mix_ref[...] = mix
    h_ref[...] = _rms(x_ref[...] + mix.astype(F32), nffn_ref[...]).astype(BF16)


def _mix_norm(x2, ret, ssm, glu_w, glu_b, w_out, norm_ffn):
    t = x2.shape[0]
    r = MIX_TILE
    tok = lambda i: (i, 0)
    return pl.pallas_call(
        _mix_norm_kernel,
        grid=(t // r,),
        in_specs=[pl.BlockSpec((r, D_MODEL), tok), pl.BlockSpec((r, RET_WIDTH), tok),
                  pl.BlockSpec((SSM_BLOCKS, SUBLANES, r // SUBLANES, LANES),
                               lambda i: (0, 0, i, 0)),
                  _const_spec((SSM_WIDTH, SSM_WIDTH)), _const_spec((1, SSM_WIDTH)),
                  _const_spec((D_MODEL, D_MODEL)), _const_spec((1, D_MODEL))],
        out_specs=[pl.BlockSpec((r, D_MODEL), tok), pl.BlockSpec((r, D_MODEL), tok)],
        out_shape=[jax.ShapeDtypeStruct((t, D_MODEL), BF16),
                   jax.ShapeDtypeStruct((t, D_MODEL), BF16)],
        scratch_shapes=[pltpu.VMEM((SSM_BLOCKS, r, LANES), F32)],
        compiler_params=pltpu.CompilerParams(
            dimension_semantics=("arbitrary",), vmem_limit_bytes=VMEM_LIMIT),
        name="mix_norm",
    )(x2, ret, ssm, glu_w, glu_b, w_out, norm_ffn)


def _conv_ffn_kernel(x_ref, mix_ref, hm_ref, hp_ref, hn_ref, w_up_ref, cw_ref, cb_ref,
                     w_down_ref, nfin_ref, o_ref, hid_scr, *, tiles_per_seq):
    r = hm_ref.shape[0]
    ext = r + 2 * HALO
    j = pl.program_id(0) % tiles_per_seq
    hp = jnp.where(j == 0, jnp.zeros(hp_ref.shape, BF16), hp_ref[...])
    hn = jnp.where(j == tiles_per_seq - 1, jnp.zeros(hn_ref.shape, BF16), hn_ref[...])
    h = jnp.concatenate([hp, hm_ref[...], hn], axis=0)
    for jj in range(D_FF // FF_TILE):
        parts = []
        for off in (jj * FF_TILE, D_FF + jj * FF_TILE):
            z = jnp.dot(h, w_up_ref[:, off:off + FF_TILE], preferred_element_type=F32)
            zc = (pltpu.roll(z, 1, 0)[HALO:HALO + r] * cw_ref[0:1, off:off + FF_TILE]
                  + z[HALO:HALO + r] * cw_ref[1:2, off:off + FF_TILE]
                  + pltpu.roll(z, ext - 1, 0)[HALO:HALO + r] * cw_ref[2:3, off:off + FF_TILE]
                  + cb_ref[:, off:off + FF_TILE])
            parts.append(zc)
        val, gate = parts
        t = jnp.tanh(gate * (GELU_C0 + (GELU_C0 * GELU_C1) * (gate * gate)))
        hid_scr[:, jj * FF_TILE:(jj + 1) * FF_TILE] = ((0.5 * gate * val) * (1.0 + t)).astype(BF16)
    x1 = x_ref[...] + mix_ref[...].astype(F32)
    x2 = x1 + jnp.dot(hid_scr[...], w_down_ref[...], preferred_element_type=F32)
    o_ref[...] = _rms(x2, nfin_ref[...])


def _conv_ffn(x, mix, h, w_up, conv_w, conv_b, w_down, norm_final, seq):
    t = x.shape[0]
    r = FFN_TILE
    assert seq % r == 0, "token tiles must not straddle sequences"
    tiles_per_seq = seq // r
    hb = r // HALO
    last = t // HALO - 1
    tok = lambda i: (i, 0)
    return pl.pallas_call(
        functools.partial(_conv_ffn_kernel, tiles_per_seq=tiles_per_seq),
        grid=(t // r,),
        in_specs=[pl.BlockSpec((r, D_MODEL), tok), pl.BlockSpec((r, D_MODEL), tok),
                  pl.BlockSpec((r, D_MODEL), tok),
                  pl.BlockSpec((HALO, D_MODEL), lambda i: (jnp.maximum(i * hb - 1, 0), 0)),
                  pl.BlockSpec((HALO, D_MODEL), lambda i: (jnp.minimum((i + 1) * hb, last), 0)),
                  _const_spec((D_MODEL, 2 * D_FF)), _const_spec((3, 2 * D_FF)),
                  _const_spec((1, 2 * D_FF)), _const_spec((D_FF, D_MODEL)),
                  _const_spec((1, D_MODEL))],
        out_specs=pl.BlockSpec((r, D_MODEL), tok),
        out_shape=jax.ShapeDtypeStruct((t, D_MODEL), F32),
        scratch_shapes=[pltpu.VMEM((r, D_FF), BF16)],
        compiler_params=pltpu.CompilerParams(
            dimension_semantics=("arbitrary",), vmem_limit_bytes=VMEM_LIMIT),
        name="conv_ffn",
    )(x, mix, h, h, h, w_up, conv_w, conv_b, w_down, norm_final)


def _rotary_tables(seq):
    half = HEAD_DIM // 2
    inv_freq = ROPE_BASE ** (-jnp.arange(half, dtype=F32) / half)
    ang = jnp.arange(seq, dtype=F32)[:, None] * inv_freq[None, :]
    cos = jnp.cos(ang)
    sin = jnp.sin(ang)
    return jnp.concatenate([cos, cos], axis=1), jnp.concatenate([-sin, sin], axis=1)


def _layer(x, prep):
    b, seq, _ = x.shape
    x2 = x.reshape(b * seq, D_MODEL)
    q, k, v, g, u = _in_proj(x2, prep["norm_mix"], prep["w_in"], prep["cos"], prep["sin"],
                             prep["gn_gain"], seq)
    ret = _retention(q, k, v, g, prep["ret_tabs"], seq)
    ssm = _s5(u, prep["s5"], seq)
    mix, h = _mix_norm(x2, ret, ssm, prep["glu_w"], prep["glu_b"], prep["w_out"], prep["norm_ffn"])
    out = _conv_ffn(x2, mix, h, prep["w_up"], prep["conv_w"], prep["conv_b"], prep["w_down"],
                    prep["norm_final"], seq)
    return out.reshape(b, seq, D_MODEL)


def kernel(x_prompt, x_sample, norm_mix, w_in, ret_gn_gain, s5_lambda_re, s5_lambda_im, s5_log_dt, s5_B_re, s5_B_im, s5_C_re, s5_C_im, s5_D, s5_glu_w, s5_glu_b, w_out, norm_ffn, w_up, conv_w, conv_b, w_down, norm_final):
    assert norm_mix.shape[0] == 1, "single-layer trunk"
    seq = x_prompt.shape[1]
    assert x_sample.shape[1] == seq
    cos, sin = _rotary_tables(seq)
    prep = {
        "norm_mix": norm_mix[0][None].astype(F32),
        "w_in": w_in[0].astype(BF16),
        "cos": cos, "sin": sin,
        "ret_tabs": _retention_tables(),
        "gn_gain": ret_gn_gain[0][None].astype(F32),
        "s5": _s5_tables(s5_lambda_re[0], s5_lambda_im[0], s5_log_dt[0], s5_B_re[0], s5_B_im[0],
                         s5_C_re[0], s5_C_im[0], s5_D[0]),
        "glu_w": s5_glu_w[0].astype(BF16),
        "glu_b": s5_glu_b[0][None].astype(F32),
        "w_out": w_out[0].astype(BF16),
        "norm_ffn": norm_ffn[0][None].astype(F32),
        "w_up": w_up[0].astype(BF16),
        "conv_w": conv_w[0].astype(F32),
        "conv_b": conv_b[0][None].astype(F32),
        "w_down": w_down[0].astype(BF16),
        "norm_final": norm_final[None].astype(F32),
    }
    return (_layer(x_prompt, prep), _layer(x_sample, prep))
```
